```python
import math
import jax, jax.numpy as jnp
from jax import lax
import numpy as np

D_MODEL = 1024
BATCH = 2
SEQ = 8192
DEPTH = 1

DA_HEADS = 4
DA_HEAD_DIM = 128
DA_V_DIM = 2 * DA_HEAD_DIM
DA_COLS = DA_HEADS * DA_V_DIM
MLA_HEADS = 8
MLA_NOPE_DIM = 128
MLA_ROPE_DIM = 64
MLA_QK_DIM = MLA_NOPE_DIM + MLA_ROPE_DIM
MLA_V_DIM = 128
MLA_Q_RANK = 384
MLA_KV_RANK = 256
MLA_COLS = MLA_HEADS * MLA_V_DIM
ROPE_THETA = 10000.0
REL_BUCKETS = 32
REL_MAX_DISTANCE = 128
PEER_HEADS = 8
PEER_N_KEYS = 128
PEER_N_EXPERTS = PEER_N_KEYS * PEER_N_KEYS
PEER_KEY_DIM = 256
PEER_HALF = PEER_KEY_DIM // 2
PEER_TOPK = 16
N_BRANCHES = 2
Q_BLOCK = 128
TOKEN_BLOCK = 128
EPS = 1e-6

DA_Q_COLS = DA_HEADS * 2 * DA_HEAD_DIM
DA_K_COLS = DA_HEADS * 2 * DA_HEAD_DIM
DA_V_COLS = DA_HEADS * DA_V_DIM
GATE_COLS = N_BRANCHES * D_MODEL
IN_SPLITS = (DA_Q_COLS, DA_K_COLS, DA_V_COLS, MLA_Q_RANK, MLA_KV_RANK, MLA_ROPE_DIM, GATE_COLS)
IN_COLS = DA_Q_COLS + DA_K_COLS + DA_V_COLS + MLA_Q_RANK + MLA_KV_RANK + MLA_ROPE_DIM + GATE_COLS

kernel_name = 'hybrid_diffattn_mla_peer_block'


def rms_norm(x, g):
    xf = x.astype(jnp.float32)
    y = xf * lax.rsqrt(jnp.mean(xf * xf, axis=-1, keepdims=True) + EPS)
    return (y * g.astype(jnp.float32)).astype(x.dtype)


def split_cols(t, sizes):
    out, off = [], 0
    for n in sizes:
        out.append(t[..., off:off + n])
        off += n
    return out


def apply_rope(x, cos, sin):
    half = x.shape[-1] // 2
    x1, x2 = x[..., :half], x[..., half:]
    return jnp.concatenate([x1 * cos - x2 * sin, x1 * sin + x2 * cos], axis=-1).astype(x.dtype)


def lambda_init_fn(layer_idx):
    return 0.8 - 0.6 * math.exp(-0.3 * layer_idx)


def t5_bucket(dist):
    n = jnp.maximum(dist, 0)
    max_exact = REL_BUCKETS // 2
    nf = jnp.maximum(n, max_exact).astype(jnp.float32)
    large = max_exact + (jnp.log(nf / max_exact) / math.log(REL_MAX_DISTANCE / max_exact)
                         * (REL_BUCKETS - max_exact)).astype(jnp.int32)
    large = jnp.minimum(large, REL_BUCKETS - 1)
    return jnp.where(n < max_exact, n, large)


def diff_attention(q, k, v, rel_bias, lam):
    seq = q.shape[3]
    scale = DA_HEAD_DIM ** -0.5
    kpos = jnp.arange(seq)

    def block(i):
        start = i * Q_BLOCK
        qb = lax.dynamic_slice_in_dim(q, start, Q_BLOCK, axis=3)
        qpos = start + jnp.arange(Q_BLOCK)
        dist = qpos[:, None] - kpos[None, :]
        bias = jnp.take(rel_bias, t5_bucket(dist), axis=0)
        logits = jnp.einsum('bhmqd,bhmkd->bhmqk', qb, k).astype(jnp.float32) * scale
        logits = logits + jnp.transpose(bias, (2, 0, 1)).astype(jnp.float32)[None, :, None]
        logits = jnp.where(dist >= 0, logits, -jnp.inf)
        p = jax.nn.softmax(logits, axis=-1)
        w = p[:, :, 0] - lam * p[:, :, 1]
        return jnp.einsum('bhqk,bhkd->bhqd', w.astype(v.dtype), v)

    out = lax.map(block, jnp.arange(seq // Q_BLOCK))
    nblk, b, h, qn, dv = out.shape
    return jnp.transpose(out, (1, 0, 3, 2, 4)).reshape(b, seq, h, dv)


def causal_attention(q, k, v):
    seq = q.shape[2]
    scale = q.shape[-1] ** -0.5
    kpos = jnp.arange(seq)

    def block(i):
        start = i * Q_BLOCK
        qb = lax.dynamic_slice_in_dim(q, start, Q_BLOCK, axis=2)
        qpos = start + jnp.arange(Q_BLOCK)
        logits = jnp.einsum('bhqd,bhkd->bhqk', qb, k).astype(jnp.float32) * scale
        logits = jnp.where(kpos[None, :] <= qpos[:, None], logits, -jnp.inf)
        p = jax.nn.softmax(logits, axis=-1)
        return jnp.einsum('bhqk,bhkd->bhqd', p.astype(v.dtype), v)

    out = lax.map(block, jnp.arange(seq // Q_BLOCK))
    nblk, b, h, qn, dv = out.shape
    return jnp.transpose(out, (1, 0, 3, 2, 4)).reshape(b, seq, h * dv)


def peer(h, w_query, sub_keys, u_table, v_table):
    b, s, d = h.shape
    q = jnp.einsum('bsd,dp->bsp', h, w_query).reshape(b, s, PEER_HEADS, 2, PEER_HALF)
    scores = jnp.einsum('bshcp,hcnp->bshcn', q, sub_keys).astype(jnp.float32)
    s1, i1 = lax.top_k(scores[..., 0, :], PEER_TOPK)
    s2, i2 = lax.top_k(scores[..., 1, :], PEER_TOPK)
    cand = (s1[..., :, None] + s2[..., None, :]).reshape(b, s, PEER_HEADS, PEER_TOPK * PEER_TOPK)
    cidx = (i1[..., :, None] * PEER_N_KEYS + i2[..., None, :]).reshape(b, s, PEER_HEADS, PEER_TOPK * PEER_TOPK)
    top, pos = lax.top_k(cand, PEER_TOPK)
    eidx = jnp.take_along_axis(cidx, pos, axis=-1)
    gate = jax.nn.softmax(top, axis=-1).astype(h.dtype)
    nblk = s // TOKEN_BLOCK

    def to_blocks(t):
        return jnp.moveaxis(t.reshape((b, nblk, TOKEN_BLOCK) + t.shape[2:]), 1, 0)

    def block(args):
        hb, ib, gb = args
        u = jnp.take(u_table, ib, axis=0)
        a = jax.nn.gelu(jnp.einsum('btd,bthkd->bthk', hb, u), approximate=False)
        vsel = jnp.take(v_table, ib, axis=0)
        return jnp.einsum('bthk,bthkd->btd', gb * a, vsel)

    out = lax.map(block, (to_blocks(h), to_blocks(eidx), to_blocks(gate)))
    return jnp.moveaxis(out, 0, 1).reshape(b, s, d)


def setup_inputs(seed: int = 0) -> dict:
    key = jax.random.key(seed)
    ks = jax.random.split(key, 26)

    def nrm(k, shape, scale):
        return jax.random.normal(k, shape, jnp.float32) * scale

    def gain(k, shape):
        return 1.0 + 0.02 * jax.random.normal(k, shape, jnp.float32)

    return {
        'x': nrm(ks[0], (BATCH, SEQ, D_MODEL), 1.0),
        'w_in': nrm(ks[1], (DEPTH, D_MODEL, IN_COLS), D_MODEL ** -0.5),
        'attn_norm': gain(ks[2], (DEPTH, D_MODEL)),
        'da_q_norm': gain(ks[3], (DEPTH, DA_HEAD_DIM)),
        'da_k_norm': gain(ks[4], (DEPTH, DA_HEAD_DIM)),
        'da_lambda_q1': nrm(ks[5], (DEPTH, DA_HEAD_DIM), 0.1),
        'da_lambda_k1': nrm(ks[6], (DEPTH, DA_HEAD_DIM), 0.1),
        'da_lambda_q2': nrm(ks[7], (DEPTH, DA_HEAD_DIM), 0.1),
        'da_lambda_k2': nrm(ks[8], (DEPTH, DA_HEAD_DIM), 0.1),
        'da_out_norm': gain(ks[9], (DEPTH, DA_V_DIM)),
        'da_proj': nrm(ks[10], (DEPTH, DA_COLS, D_MODEL), DA_COLS ** -0.5),
        'mla_q_latent_norm': gain(ks[11], (DEPTH, MLA_Q_RANK)),
        'mla_w_uq': nrm(ks[12], (DEPTH, MLA_Q_RANK, MLA_HEADS * MLA_QK_DIM), MLA_Q_RANK ** -0.5),
        'mla_kv_latent_norm': gain(ks[13], (DEPTH, MLA_KV_RANK)),
        'mla_w_ukv': nrm(ks[14], (DEPTH, MLA_KV_RANK, MLA_HEADS * (MLA_NOPE_DIM + MLA_V_DIM)), MLA_KV_RANK ** -0.5),
        'mla_q_norm': gain(ks[15], (DEPTH, MLA_QK_DIM)),
        'mla_k_norm': gain(ks[16], (DEPTH, MLA_QK_DIM)),
        'mla_proj': nrm(ks[17], (DEPTH, MLA_COLS, D_MODEL), MLA_COLS ** -0.5),
        'w_out': nrm(ks[18], (DEPTH, D_MODEL, D_MODEL), D_MODEL ** -0.5),
        'rel_bias': nrm(ks[19], (REL_BUCKETS, DA_HEADS), 0.5),
        'ffn_norm': gain(ks[20], (DEPTH, D_MODEL)),
        'peer_w_query': nrm(ks[21], (DEPTH, D_MODEL, PEER_HEADS * PEER_KEY_DIM), D_MODEL ** -0.5),
        'peer_sub_keys': nrm(ks[22], (DEPTH, PEER_HEADS, 2, PEER_N_KEYS, PEER_HALF), PEER_HALF ** -0.5),
        'peer_u': nrm(ks[23], (DEPTH, PEER_N_EXPERTS, D_MODEL), D_MODEL ** -0.5),
        'peer_v': nrm(ks[24], (DEPTH, PEER_N_EXPERTS, D_MODEL), 0.5),
    }


def reference(x, w_in, attn_norm, da_q_norm, da_k_norm, da_lambda_q1, da_lambda_k1, da_lambda_q2,
              da_lambda_k2, da_out_norm, da_proj, mla_q_latent_norm, mla_w_uq, mla_kv_latent_norm,
              mla_w_ukv, mla_q_norm, mla_k_norm, mla_proj, w_out, rel_bias, ffn_norm, peer_w_query,
              peer_sub_keys, peer_u, peer_v):
    b, s, _ = x.shape
    pos = jnp.arange(s, dtype=jnp.float32)
    inv_freq = ROPE_THETA ** (-jnp.arange(0, MLA_ROPE_DIM, 2, dtype=jnp.float32) / MLA_ROPE_DIM)
    ang = pos[:, None] * inv_freq[None, :]
    cos, sin = jnp.cos(ang), jnp.sin(ang)

    for l in range(DEPTH):
        h = rms_norm(x, attn_norm[l])
        proj = jnp.einsum('bsd,dc->bsc', h, w_in[l])
        qa, ka, va, cq, ckv, kr, gates = split_cols(proj, IN_SPLITS)

        qa = rms_norm(qa.reshape(b, s, DA_HEADS, 2, DA_HEAD_DIM), da_q_norm[l])
        ka = rms_norm(ka.reshape(b, s, DA_HEADS, 2, DA_HEAD_DIM), da_k_norm[l])
        qa = jnp.transpose(qa, (0, 2, 3, 1, 4))
        ka = jnp.transpose(ka, (0, 2, 3, 1, 4))
        va = jnp.transpose(va.reshape(b, s, DA_HEADS, DA_V_DIM), (0, 2, 1, 3))
        lam_init = lambda_init_fn(l)
        lam = (jnp.exp(jnp.sum(da_lambda_q1[l].astype(jnp.float32) * da_lambda_k1[l].astype(jnp.float32)))
               - jnp.exp(jnp.sum(da_lambda_q2[l].astype(jnp.float32) * da_lambda_k2[l].astype(jnp.float32)))
               + lam_init)
        ya = diff_attention(qa, ka, va, rel_bias, lam)
        ya = rms_norm(ya, da_out_norm[l]) * (1.0 - lam_init)
        ya = jnp.einsum('bsc,cd->bsd', ya.reshape(b, s, DA_COLS), da_proj[l])

        cq = rms_norm(cq, mla_q_latent_norm[l])
        qb = jnp.einsum('bsr,rc->bsc', cq, mla_w_uq[l]).reshape(b, s, MLA_HEADS, MLA_QK_DIM)
        q_nope, q_rope = qb[..., :MLA_NOPE_DIM], qb[..., MLA_NOPE_DIM:]
        q_rope = apply_rope(q_rope, cos[:, None, :], sin[:, None, :])
        ckv = rms_norm(ckv, mla_kv_latent_norm[l])
        kv = jnp.einsum('bsr,rc->bsc', ckv, mla_w_ukv[l]).reshape(b, s, MLA_HEADS, MLA_NOPE_DIM + MLA_V_DIM)
        k_nope, vb = kv[..., :MLA_NOPE_DIM], kv[..., MLA_NOPE_DIM:]
        k_rope = apply_rope(kr, cos, sin)
        k_rope = jnp.broadcast_to(k_rope[:, :, None, :], (b, s, MLA_HEADS, MLA_ROPE_DIM))
        qb = rms_norm(jnp.concatenate([q_nope, q_rope], axis=-1), mla_q_norm[l])
        kb = rms_norm(jnp.concatenate([k_nope, k_rope], axis=-1), mla_k_norm[l])
        yb = causal_attention(jnp.transpose(qb, (0, 2, 1, 3)), jnp.transpose(kb, (0, 2, 1, 3)),
                              jnp.transpose(vb, (0, 2, 1, 3)))
        yb = jnp.einsum('bsc,cd->bsd', yb, mla_proj[l])

        g = jax.nn.sigmoid(gates.reshape(b, s, N_BRANCHES, D_MODEL))
        merged = g[:, :, 0] * ya + g[:, :, 1] * yb
        x = x + jnp.einsum('bsd,de->bse', merged, w_out[l])

        h2 = rms_norm(x, ffn_norm[l])
        x = x + peer(h2, peer_w_query[l], peer_sub_keys[l], peer_u[l], peer_v[l])
    return x
```

```python
import functools
import math

import numpy as np
import jax
import jax.numpy as jnp
from jax import lax
from jax.experimental import pallas as pl
from jax.experimental.pallas import tpu as pltpu

F32 = jnp.float32
BF16 = jnp.bfloat16

EPS = 1e-6
DA_HEADS = 4
DA_HEAD_DIM = 128
DA_V_DIM = 256
MLA_HEADS = 8
MLA_NOPE = 128
MLA_ROPE = 64
MLA_QK = MLA_NOPE + MLA_ROPE
MLA_QK_PAD = 256
MLA_V = 128
MLA_Q_RANK = 384
MLA_KV_RANK = 256
ROPE_THETA = 10000.0
REL_BUCKETS = 32
REL_MAX_DISTANCE = 128
PEER_HEADS = 8
PEER_KEYS = 128
PEER_TOPK = 16
LAMBDA_INIT = 0.8 - 0.6 * math.exp(-0.3 * 0)
MASK_VALUE = -1e30
INV_SQRT2 = 0.7071067811865476

VMEM_LIMIT = 56 * 1024 * 1024


def _cparams(*sem):
    return pltpu.CompilerParams(dimension_semantics=sem, vmem_limit_bytes=VMEM_LIMIT)


def _tile(n, pref):
    t = min(n, pref)
    assert n % t == 0, (n, t)
    return t


def _norm_matmul_kernel(x_ref, g_ref, w_ref, o_ref, xn_ref):
    @pl.when(pl.program_id(1) == 0)
    def _():
        x = x_ref[...]
        ms = jnp.mean(x * x, axis=-1, keepdims=True)
        xn_ref[...] = (x * lax.rsqrt(ms + EPS) * g_ref[...]).astype(BF16)

    o_ref[...] = jnp.dot(xn_ref[...], w_ref[...], preferred_element_type=F32).astype(o_ref.dtype)


def norm_matmul(x, g, w, out_dtype=F32, tm=1024, tn=512):
    t, k = x.shape
    n = w.shape[1]
    tm, tn = _tile(t, tm), _tile(n, tn)
    return pl.pallas_call(
        _norm_matmul_kernel,
        grid=(t // tm, n // tn),
        in_specs=[pl.BlockSpec((tm, k), lambda i, j: (i, 0)),
                  pl.BlockSpec((1, k), lambda i, j: (0, 0)),
                  pl.BlockSpec((k, tn), lambda i, j: (0, j))],
        out_specs=pl.BlockSpec((tm, tn), lambda i, j: (i, j)),
        out_shape=jax.ShapeDtypeStruct((t, n), out_dtype),
        scratch_shapes=[pltpu.VMEM((tm, k), BF16)],
        compiler_params=_cparams("parallel", "arbitrary"),
        name="norm_matmul",
    )(x, g.reshape(1, k).astype(F32), w)


def _matmul_kernel(x_ref, w_ref, o_ref):
    o_ref[...] = jnp.dot(x_ref[...], w_ref[...], preferred_element_type=F32).astype(o_ref.dtype)


def matmul(x, w, out_dtype=F32, tm=1024, tn=512):
    t, k = x.shape
    n = w.shape[1]
    tm, tn = _tile(t, tm), _tile(n, tn)
    return pl.pallas_call(
        _matmul_kernel,
        grid=(t // tm, n // tn),
        in_specs=[pl.BlockSpec((tm, k), lambda i, j: (i, 0)),
                  pl.BlockSpec((k, tn), lambda i, j: (0, j))],
        out_specs=pl.BlockSpec((tm, tn), lambda i, j: (i, j)),
        out_shape=jax.ShapeDtypeStruct((t, n), out_dtype),
        compiler_params=_cparams("parallel", "arbitrary"),
        name="matmul",
    )(x, w)


def _softmax_step(s, v, m_ref, l_ref, acc_ref, idx):
    m_old = m_ref[idx]
    m_new = jnp.maximum(m_old, jnp.max(s, axis=-1, keepdims=True))
    p = jnp.exp(s - m_new)
    alpha = jnp.exp(m_old - m_new)
    l_ref[idx] = alpha * l_ref[idx] + jnp.sum(p, axis=-1, keepdims=True)
    acc_ref[idx] = alpha * acc_ref[idx] + jnp.dot(p.astype(BF16), v, preferred_element_type=F32)
    m_ref[idx] = m_new


def _da_attn_kernel(lam_ref, q_ref, k_ref, v_ref, bias_ref, g_ref, o_ref, m_ref, l_ref, acc_ref, *, blk):
    qi = pl.program_id(2)
    m_ref[...] = jnp.full(m_ref.shape, MASK_VALUE, F32)
    l_ref[...] = jnp.zeros(l_ref.shape, F32)
    acc_ref[...] = jnp.zeros(acc_ref.shape, F32)

    def step(j, bias_idx):
        start = pl.multiple_of(j * blk, blk)
        kblk = k_ref[0, pl.ds(start, blk), :]
        vblk = v_ref[0, pl.ds(start, blk), :]
        for mp in range(2):
            q = q_ref[0, :, mp * DA_HEAD_DIM:(mp + 1) * DA_HEAD_DIM]
            k = kblk[:, mp * DA_HEAD_DIM:(mp + 1) * DA_HEAD_DIM]
            s = lax.dot_general(q, k, (((1,), (1,)), ((), ())), preferred_element_type=F32)
            if bias_idx is not None:
                s = s + bias_ref[0, bias_idx]
            _softmax_step(s, vblk, m_ref, l_ref, acc_ref, mp)

    def far_body(j, carry):
        step(j, None)
        return carry

    lax.fori_loop(0, jnp.maximum(qi - 1, 0), far_body, 0)

    @pl.when(qi >= 1)
    def _():
        step(qi - 1, 1)

    step(qi, 0)

    o = acc_ref[0] / l_ref[0] - lam_ref[0] * (acc_ref[1] / l_ref[1])
    ms = jnp.mean(o * o, axis=-1, keepdims=True)
    o_ref[0] = (o * lax.rsqrt(ms + EPS) * g_ref[...]).astype(o_ref.dtype)


def da_attention(q, k, v, bias_tiles, lam, out_gain, blk):
    b, s, _ = q.shape
    hw = 2 * DA_HEAD_DIM
    kern = functools.partial(_da_attn_kernel, blk=blk)
    return pl.pallas_call(
        kern,
        grid=(b, DA_HEADS, s // blk),
        in_specs=[pl.BlockSpec(memory_space=pltpu.SMEM),
                  pl.BlockSpec((1, blk, hw), lambda bi, h, i: (bi, i, h)),
                  pl.BlockSpec((1, s, hw), lambda bi, h, i: (bi, 0, h)),
                  pl.BlockSpec((1, s, DA_V_DIM), lambda bi, h, i: (bi, 0, h)),
                  pl.BlockSpec((1, 2, blk, blk), lambda bi, h, i: (h, 0, 0, 0)),
                  pl.BlockSpec((1, DA_V_DIM), lambda bi, h, i: (0, 0))],
        out_specs=pl.BlockSpec((1, blk, DA_V_DIM), lambda bi, h, i: (bi, i, h)),
        out_shape=jax.ShapeDtypeStruct((b, s, DA_HEADS * DA_V_DIM), BF16),
        scratch_shapes=[pltpu.VMEM((2, blk, 1), F32), pltpu.VMEM((2, blk, 1), F32),
                        pltpu.VMEM((2, blk, DA_V_DIM), F32)],
        compiler_params=_cparams("parallel", "parallel", "arbitrary"),
        name="da_attention",
    )(lam, q, k, v, bias_tiles, out_gain)


def _mla_attn_kernel(q_ref, k_ref, v_ref, mask_ref, o_ref, m_ref, l_ref, acc_ref, *, blk):
    qi = pl.program_id(2)
    m_ref[...] = jnp.full(m_ref.shape, MASK_VALUE, F32)
    l_ref[...] = jnp.zeros(l_ref.shape, F32)
    acc_ref[...] = jnp.zeros(acc_ref.shape, F32)

    def step(j, masked):
        start = pl.multiple_of(j * blk, blk)
        k = k_ref[0, 0, pl.ds(start, blk), :]
        v = v_ref[0, 0, pl.ds(start, blk), :]
        s = lax.dot_general(q_ref[0, 0], k, (((1,), (1,)), ((), ())), preferred_element_type=F32)
        if masked:
            s = s + mask_ref[...]
        _softmax_step(s, v, m_ref, l_ref, acc_ref, 0)

    def far_body(j, carry):
        step(j, False)
        return carry

    lax.fori_loop(0, qi, far_body, 0)
    step(qi, True)
    o_ref[0] = (acc_ref[0] / l_ref[0]).astype(o_ref.dtype)


def mla_attention(q, k, v, mask_tile, blk):
    b, h, s, _ = q.shape
    kern = functools.partial(_mla_attn_kernel, blk=blk)
    return pl.pallas_call(
        kern,
        grid=(b, h, s // blk),
        in_specs=[pl.BlockSpec((1, 1, blk, MLA_QK_PAD), lambda bi, hi, i: (bi, hi, i, 0)),
                  pl.BlockSpec((1, 1, s, MLA_QK_PAD), lambda bi, hi, i: (bi, hi, 0, 0)),
                  pl.BlockSpec((1, 1, s, MLA_V), lambda bi, hi, i: (bi, hi, 0, 0)),
                  pl.BlockSpec((blk, blk), lambda bi, hi, i: (0, 0))],
        out_specs=pl.BlockSpec((1, blk, MLA_V), lambda bi, hi, i: (bi, i, hi)),
        out_shape=jax.ShapeDtypeStruct((b, s, h * MLA_V), BF16),
        scratch_shapes=[pltpu.VMEM((1, blk, 1), F32), pltpu.VMEM((1, blk, 1), F32),
                        pltpu.VMEM((1, blk, MLA_V), F32)],
        compiler_params=_cparams("parallel", "parallel", "arbitrary"),
        name="mla_attention",
    )(q, k, v, mask_tile)


def _first_max(cur, row_ids, n_rows):
    m = jnp.max(cur, axis=0, keepdims=True)
    first = jnp.min(jnp.where(cur == m, row_ids, n_rows), axis=0, keepdims=True)
    return m, row_ids == first


def _top16(s, want_rank):
    n_rows, tm = s.shape
    row_ids = lax.broadcasted_iota(jnp.int32, (n_rows, tm), 0)
    slot_ids = lax.broadcasted_iota(jnp.int32, (PEER_TOPK, tm), 0)
    cur = s
    rank = jnp.full(s.shape, float(PEER_TOPK), F32)
    vals = jnp.zeros((PEER_TOPK, tm), F32)
    for a in range(PEER_TOPK):
        m, sel = _first_max(cur, row_ids, n_rows)
        if want_rank:
            rank = jnp.where(sel, float(a), rank)
        cur = jnp.where(sel, -jnp.inf, cur)
        vals = jnp.where(slot_ids == a, m, vals)
    return vals, (rank if want_rank else cur)


def _peer_select_kernel(q_ref, keys_ref, n_ref, e1_ref, k_ref, e2_ref):
    tm = q_ref.shape[0]
    q = q_ref[...]

    def scores(c):
        return lax.dot_general(keys_ref[0, c], q[:, c * PEER_KEYS:(c + 1) * PEER_KEYS],
                               (((1,), (1,)), ((), ())), precision=lax.Precision.HIGHEST,
                               preferred_element_type=F32)

    s1, s2 = scores(0), scores(1)
    v1, rank1 = _top16(s1, True)
    v2, rank2 = _top16(s2, True)

    row8 = lax.broadcasted_iota(jnp.int32, (8, tm), 0)
    groups = [v1[0:1] + v2, v1[1:2] + v2[0:8]]
    for a in range(2, 8):
        groups.append(jnp.where(row8 < PEER_TOPK // (a + 1), v1[a:a + 1] + v2[0:8], -jnp.inf))
    groups.append(v1[8:16] + v2[0:1])
    cand = jnp.concatenate(groups, axis=0)
    n_cand = cand.shape[0]
    cand_ids = lax.broadcasted_iota(jnp.int32, (n_cand, tm), 0)
    cmax = v1[0:1] + v2[0:1]
    z = jnp.zeros((1, tm), F32)
    picked = jnp.zeros((n_cand, tm), F32)
    cur = cand
    for _ in range(PEER_TOPK):
        m, sel = _first_max(cur, cand_ids, n_cand)
        z = z + jnp.exp(m - cmax)
        picked = jnp.where(sel, 1.0, picked)
        cur = jnp.where(sel, -jnp.inf, cur)

    n_rows = [jnp.sum(picked[0:16], axis=0, keepdims=True), jnp.sum(picked[16:24], axis=0, keepdims=True)]
    for a in range(2, 8):
        lo = 24 + (a - 2) * 8
        n_rows.append(jnp.sum(picked[lo:lo + 8], axis=0, keepdims=True))
    n_sel = jnp.concatenate(n_rows + [picked[72:80]], axis=0)

    n_of_i = jnp.zeros(s1.shape, F32)
    for a in range(PEER_TOPK):
        n_of_i = jnp.where(rank1 == float(a), n_sel[a:a + 1], n_of_i)

    n_ref[0] = n_of_i
    k_ref[0] = rank2
    e1_ref[0] = jnp.exp(s1 - v1[0:1])
    e2_ref[0] = jnp.exp(s2 - v2[0:1]) / z


def peer_select(q, sub_keys, tm=512):
    t = q.shape[0]
    tm = _tile(t, tm)
    out = jax.ShapeDtypeStruct((PEER_HEADS, PEER_KEYS, t), F32)
    ospec = pl.BlockSpec((1, PEER_KEYS, tm), lambda i, h: (h, 0, i))
    return pl.pallas_call(
        _peer_select_kernel,
        grid=(t // tm, PEER_HEADS),
        in_specs=[pl.BlockSpec((tm, 2 * PEER_KEYS), lambda i, h: (i, h)),
                  pl.BlockSpec((1, 2, PEER_KEYS, PEER_KEYS), lambda i, h: (h, 0, 0, 0))],
        out_specs=[ospec, ospec, ospec, ospec],
        out_shape=[out, out, out, out],
        compiler_params=_cparams("parallel", "arbitrary"),
        name="peer_select",
    )(q, sub_keys)


def _peer_expert_kernel(ht_ref, u_ref, vt_ref, n_ref, e1_ref, k_ref, e2_ref, o_ref, acc_ref, *, n_sub):
    eb = pl.program_id(1)

    @pl.when(eb == 0)
    def _():
        acc_ref[...] = jnp.zeros(acc_ref.shape, F32)

    a = jnp.dot(u_ref[...], ht_ref[...], preferred_element_type=F32)
    parts = []
    for sub in range(n_sub):
        i = eb * n_sub + sub
        a_sub = a[sub * PEER_KEYS:(sub + 1) * PEER_KEYS]
        act = 0.5 * a_sub * (1.0 + lax.erf(a_sub * INV_SQRT2))
        gate = jnp.zeros(a_sub.shape, F32)
        for h in range(PEER_HEADS):
            n_row = n_ref[h, pl.ds(i, 1), :]
            e1_row = e1_ref[h, pl.ds(i, 1), :]
            gate = gate + jnp.where(k_ref[h] < n_row, e2_ref[h] * e1_row, 0.0)
        parts.append((gate * act).astype(BF16))
    w = jnp.concatenate(parts, axis=0)
    acc_ref[...] += jnp.dot(vt_ref[...], w, preferred_element_type=F32)

    @pl.when(eb == pl.num_programs(1) - 1)
    def _():
        o_ref[...] = acc_ref[...]


def peer_experts(ht, u, vt, n_i, e1, k_j, e2, tm=512, te=512):
    d, t = ht.shape
    e = u.shape[0]
    tm, te = _tile(t, tm), _tile(e, te)
    n_sub = te // PEER_KEYS
    tspec = pl.BlockSpec((PEER_HEADS, PEER_KEYS, tm), lambda i, j: (0, 0, i))
    kern = functools.partial(_peer_expert_kernel, n_sub=n_sub)
    return pl.pallas_call(
        kern,
        grid=(t // tm, e // te),
        in_specs=[pl.BlockSpec((d, tm), lambda i, j: (0, i)),
                  pl.BlockSpec((te, d), lambda i, j: (j, 0)),
                  pl.BlockSpec((d, te), lambda i, j: (0, j)),
                  tspec, tspec, tspec, tspec],
        out_specs=pl.BlockSpec((d, tm), lambda i, j: (0, i)),
        out_shape=jax.ShapeDtypeStruct((d, t), F32),
        scratch_shapes=[pltpu.VMEM((d, tm), F32)],
        compiler_params=_cparams("parallel", "arbitrary"),
        name="peer_experts",
    )(ht, u, vt, n_i, e1, k_j, e2)


def _rms(x, g):
    return x * lax.rsqrt(jnp.mean(x * x, axis=-1, keepdims=True) + EPS) * g


def _rope(x, cos, sin):
    half = x.shape[-1] // 2
    x1, x2 = x[..., :half], x[..., half:]
    return jnp.concatenate([x1 * cos - x2 * sin, x1 * sin + x2 * cos], axis=-1)


def _t5_bucket(dist):
    n = jnp.maximum(dist, 0)
    max_exact = REL_BUCKETS // 2
    nf = jnp.maximum(n, max_exact).astype(F32)
    large = max_exact + (jnp.log(nf / max_exact) / math.log(REL_MAX_DISTANCE / max_exact)
                         * (REL_BUCKETS - max_exact)).astype(jnp.int32)
    large = jnp.minimum(large, REL_BUCKETS - 1)
    return jnp.where(n < max_exact, n, large)


def _bias_tiles(rel_bias, blk):
    assert blk >= REL_MAX_DISTANCE
    r = np.arange(blk)[:, None] - np.arange(blk)[None, :]
    dist = jnp.asarray(np.stack([r, r + blk]))
    bias = jnp.take(rel_bias, _t5_bucket(dist), axis=0)
    bias = bias - rel_bias[REL_BUCKETS - 1]
    bias = jnp.where((dist >= 0)[..., None], bias, MASK_VALUE)
    return jnp.transpose(bias, (3, 0, 1, 2)).astype(F32)


def kernel(x, w_in, attn_norm, da_q_norm, da_k_norm, da_lambda_q1, da_lambda_k1, da_lambda_q2, da_lambda_k2, da_out_norm, da_proj, mla_q_latent_norm, mla_w_uq, mla_kv_latent_norm, mla_w_ukv, mla_q_norm, mla_k_norm, mla_proj, w_out, rel_bias, ffn_norm, peer_w_query, peer_sub_keys, peer_u, peer_v):
    b, s, d = x.shape
    t = b * s
    blk = _tile(s, 512)
    x2 = x.reshape(t, d)

    pos = jnp.arange(s, dtype=F32)
    inv_freq = ROPE_THETA ** (-jnp.arange(0, MLA_ROPE, 2, dtype=F32) / MLA_ROPE)
    ang = pos[:, None] * inv_freq[None, :]
    cos, sin = jnp.cos(ang), jnp.sin(ang)

    l = 0
    in_cols = w_in.shape[-1]
    pad_cols = (-in_cols) % 512
    w_in_p = jnp.pad(w_in[l], ((0, 0), (0, pad_cols))).astype(BF16)
    proj = norm_matmul(x2, attn_norm[l], w_in_p)
    c0 = DA_HEADS * 2 * DA_HEAD_DIM
    c1 = c0 + DA_HEADS * 2 * DA_HEAD_DIM
    c2 = c1 + DA_HEADS * DA_V_DIM
    c3 = c2 + MLA_Q_RANK
    c4 = c3 + MLA_KV_RANK
    c5 = c4 + MLA_ROPE
    qa, ka, va = proj[:, :c0], proj[:, c0:c1], proj[:, c1:c2]
    cq, ckv, kr, gates = proj[:, c2:c3], proj[:, c3:c4], proj[:, c4:c5], proj[:, c5:in_cols]

    qa = _rms(qa.reshape(b, s, DA_HEADS * 2, DA_HEAD_DIM), da_q_norm[l]) * (DA_HEAD_DIM ** -0.5)
    ka = _rms(ka.reshape(b, s, DA_HEADS * 2, DA_HEAD_DIM), da_k_norm[l])
    qa = qa.reshape(b, s, c0).astype(BF16)
    ka = ka.reshape(b, s, c0).astype(BF16)
    va = va.reshape(b, s, DA_HEADS * DA_V_DIM).astype(BF16)
    lam = (jnp.exp(jnp.sum(da_lambda_q1[l] * da_lambda_k1[l]))
           - jnp.exp(jnp.sum(da_lambda_q2[l] * da_lambda_k2[l])) + LAMBDA_INIT).reshape(1).astype(F32)
    out_gain = (da_out_norm[l] * (1.0 - LAMBDA_INIT)).reshape(1, DA_V_DIM).astype(F32)
    ya = da_attention(qa, ka, va, _bias_tiles(rel_bias, blk), lam, out_gain, blk)
    ya = matmul(ya.reshape(t, -1), da_proj[l].astype(BF16))

    cqn = _rms(cq, mla_q_latent_norm[l]).astype(BF16)
    qb = matmul(cqn, mla_w_uq[l].astype(BF16), tn=MLA_HEADS * MLA_QK).reshape(b, s, MLA_HEADS, MLA_QK)
    q_rope = _rope(qb[..., MLA_NOPE:], cos[:, None, :], sin[:, None, :])
    qb = _rms(jnp.concatenate([qb[..., :MLA_NOPE], q_rope], axis=-1), mla_q_norm[l]) * (MLA_QK ** -0.5)
    ckvn = _rms(ckv, mla_kv_latent_norm[l]).astype(BF16)
    kv = matmul(ckvn, mla_w_ukv[l].astype(BF16)).reshape(b, s, MLA_HEADS, MLA_NOPE + MLA_V)
    k_rope = _rope(kr.reshape(b, s, MLA_ROPE), cos, sin)
    k_rope = jnp.broadcast_to(k_rope[:, :, None, :], (b, s, MLA_HEADS, MLA_ROPE))
    kb = _rms(jnp.concatenate([kv[..., :MLA_NOPE], k_rope], axis=-1), mla_k_norm[l])
    zpad = ((0, 0), (0, 0), (0, 0), (0, MLA_QK_PAD - MLA_QK))
    qb = jnp.transpose(jnp.pad(qb, zpad), (0, 2, 1, 3)).astype(BF16)
    kb = jnp.transpose(jnp.pad(kb, zpad), (0, 2, 1, 3)).astype(BF16)
    vb = jnp.transpose(kv[..., MLA_NOPE:], (0, 2, 1, 3)).astype(BF16)
    causal = np.arange(blk)[:, None] >= np.arange(blk)[None, :]
    mask_tile = jnp.asarray(np.where(causal, 0.0, MASK_VALUE).astype(np.float32))
    yb = mla_attention(qb, kb, vb, mask_tile, blk)
    yb = matmul(yb.reshape(t, -1), mla_proj[l].astype(BF16))

    g = jax.nn.sigmoid(gates)
    merged = (g[:, :d] * ya + g[:, d:] * yb).astype(BF16)
    x1 = x2 + matmul(merged, w_out[l].astype(BF16))

    qp = norm_matmul(x1, ffn_norm[l], peer_w_query[l].astype(BF16))
    ht = jnp.transpose(_rms(x1, ffn_norm[l]).astype(BF16))
    n_i, e1, k_j, e2 = peer_select(qp, peer_sub_keys[l])
    u = peer_u[l].astype(BF16)
    vt = jnp.transpose(peer_v[l]).astype(BF16)
    out_t = peer_experts(ht, u, vt, n_i, e1, k_j, e2)
    return (x1 + jnp.transpose(out_t)).reshape(b, s, d)
```

```python
import functools
import math

import numpy as np
import jax
import jax.numpy as jnp
from jax import lax
from jax.experimental import pallas as pl
from jax.experimental.pallas import tpu as pltpu

F32 = jnp.float32
BF16 = jnp.bfloat16

EPS = 1e-6
DA_HEADS = 4
DA_HEAD_DIM = 128
DA_V_DIM = 256
MLA_HEADS = 8
MLA_NOPE = 128
MLA_ROPE = 64
MLA_QK = MLA_NOPE + MLA_ROPE
MLA_QK_PAD = 256
MLA_V = 128
MLA_Q_RANK = 384
MLA_KV_RANK = 256
ROPE_THETA = 10000.0
REL_BUCKETS = 32
REL_MAX_DISTANCE = 128
PEER_HEADS = 8
PEER_KEYS = 128
PEER_TOPK = 16
LAMBDA_INIT = 0.8 - 0.6 * math.exp(-0.3 * 0)
MASK_VALUE = -1e30
INV_SQRT2 = 0.7071067811865476
LOG2E = 1.4426950408889634
STAT_LANES = 128
EXPERT_CHAIN = 256
GATE_ROWS = 16

VMEM_LIMIT = 56 * 1024 * 1024


def _cparams(*sem):
    return pltpu.CompilerParams(dimension_semantics=sem, vmem_limit_bytes=VMEM_LIMIT)


def _tile(n, pref):
    t = min(n, pref)
    assert n % t == 0, (n, t)
    return t


def _norm_matmul_kernel(x_ref, g_ref, w_ref, o_ref, xn_ref):
    @pl.when(pl.program_id(1) == 0)
    def _():
        x = x_ref[...]
        ms = jnp.mean(x * x, axis=-1, keepdims=True)
        xn_ref[...] = (x * lax.rsqrt(ms + EPS) * g_ref[...]).astype(BF16)

    o_ref[...] = jnp.dot(xn_ref[...], w_ref[...], preferred_element_type=F32).astype(o_ref.dtype)


def norm_matmul(x, g, w, out_dtype=F32, tm=1024, tn=512):
    t, k = x.shape
    n = w.shape[1]
    tm, tn = _tile(t, tm), _tile(n, tn)
    return pl.pallas_call(
        _norm_matmul_kernel,
        grid=(t // tm, n // tn),
        in_specs=[pl.BlockSpec((tm, k), lambda i, j: (i, 0)),
                  pl.BlockSpec((1, k), lambda i, j: (0, 0)),
                  pl.BlockSpec((k, tn), lambda i, j: (0, j))],
        out_specs=pl.BlockSpec((tm, tn), lambda i, j: (i, j)),
        out_shape=jax.ShapeDtypeStruct((t, n), out_dtype),
        scratch_shapes=[pltpu.VMEM((tm, k), BF16)],
        compiler_params=_cparams("parallel", "arbitrary"),
        name="norm_matmul",
    )(x, g.reshape(1, k).astype(F32), w)


def _matmul_kernel(x_ref, w_ref, o_ref):
    o_ref[...] = jnp.dot(x_ref[...], w_ref[...], preferred_element_type=F32).astype(o_ref.dtype)


def matmul(x, w, out_dtype=F32, tm=1024, tn=512):
    t, k = x.shape
    n = w.shape[1]
    tm, tn = _tile(t, tm), _tile(n, tn)
    return pl.pallas_call(
        _matmul_kernel,
        grid=(t // tm, n // tn),
        in_specs=[pl.BlockSpec((tm, k), lambda i, j: (i, 0)),
                  pl.BlockSpec((k, tn), lambda i, j: (0, j))],
        out_specs=pl.BlockSpec((tm, tn), lambda i, j: (i, j)),
        out_shape=jax.ShapeDtypeStruct((t, n), out_dtype),
        compiler_params=_cparams("parallel", "arbitrary"),
        name="matmul",
    )(x, w)


def _softmax_step(s, v, m_ref, l_ref, acc_ref, idx):
    tk, dv = s.shape[1], v.shape[1]
    m_old = m_ref[idx]
    m_new = jnp.maximum(m_old, jnp.max(s, axis=-1, keepdims=True))
    p = jnp.exp2(s - pltpu.repeat(m_new, tk // STAT_LANES, axis=1))
    alpha = jnp.exp2(m_old - m_new)
    p_lanes = p[:, :STAT_LANES]
    for c in range(1, tk // STAT_LANES):
        p_lanes = p_lanes + p[:, c * STAT_LANES:(c + 1) * STAT_LANES]
    l_ref[idx] = alpha * l_ref[idx] + p_lanes
    acc_ref[idx] = (pltpu.repeat(alpha, dv // STAT_LANES, axis=1) * acc_ref[idx]
                    + jnp.dot(p.astype(BF16), v, preferred_element_type=F32))
    m_ref[idx] = m_new


def _da_attn_kernel(lam_ref, q_ref, k_ref, v_ref, bias_ref, g_ref, o_ref, m_ref, l_ref, acc_ref, *, blk):
    qi = pl.program_id(2)
    m_ref[...] = jnp.full(m_ref.shape, MASK_VALUE, F32)
    l_ref[...] = jnp.zeros(l_ref.shape, F32)
    acc_ref[...] = jnp.zeros(acc_ref.shape, F32)

    def step(j, bias_idx):
        start = pl.multiple_of(j * blk, blk)
        kblk = k_ref[0, pl.ds(start, blk), :]
        vblk = v_ref[0, pl.ds(start, blk), :]
        for mp in range(2):
            q = q_ref[0, :, mp * DA_HEAD_DIM:(mp + 1) * DA_HEAD_DIM]
            k = kblk[:, mp * DA_HEAD_DIM:(mp + 1) * DA_HEAD_DIM]
            s = lax.dot_general(q, k, (((1,), (1,)), ((), ())), preferred_element_type=F32)
            if bias_idx is not None:
                s = s + bias_ref[0, bias_idx]
            _softmax_step(s, vblk, m_ref, l_ref, acc_ref, mp)

    def far_body(j, carry):
        step(j, None)
        return carry

    lax.fori_loop(0, jnp.maximum(qi - 1, 0), far_body, 0)

    @pl.when(qi >= 1)
    def _():
        step(qi - 1, 1)

    step(qi, 0)

    o = (acc_ref[0] * (1.0 / jnp.sum(l_ref[0], axis=-1, keepdims=True))
         - acc_ref[1] * (lam_ref[0] / jnp.sum(l_ref[1], axis=-1, keepdims=True)))
    ms = jnp.mean(o * o, axis=-1, keepdims=True)
    o_ref[0] = (o * lax.rsqrt(ms + EPS) * g_ref[...]).astype(o_ref.dtype)


def da_attention(q, k, v, bias_tiles, lam, out_gain, blk):
    b, s, _ = q.shape
    hw = 2 * DA_HEAD_DIM
    kern = functools.partial(_da_attn_kernel, blk=blk)
    return pl.pallas_call(
        kern,
        grid=(b, DA_HEADS, s // blk),
        in_specs=[pl.BlockSpec(memory_space=pltpu.SMEM),
                  pl.BlockSpec((1, blk, hw), lambda bi, h, i: (bi, i, h)),
                  pl.BlockSpec((1, s, hw), lambda bi, h, i: (bi, 0, h)),
                  pl.BlockSpec((1, s, DA_V_DIM), lambda bi, h, i: (bi, 0, h)),
                  pl.BlockSpec((1, 2, blk, blk), lambda bi, h, i: (h, 0, 0, 0)),
                  pl.BlockSpec((1, DA_V_DIM), lambda bi, h, i: (0, 0))],
        out_specs=pl.BlockSpec((1, blk, DA_V_DIM), lambda bi, h, i: (bi, i, h)),
        out_shape=jax.ShapeDtypeStruct((b, s, DA_HEADS * DA_V_DIM), BF16),
        scratch_shapes=[pltpu.VMEM((2, blk, STAT_LANES), F32), pltpu.VMEM((2, blk, STAT_LANES), F32),
                        pltpu.VMEM((2, blk, DA_V_DIM), F32)],
        compiler_params=_cparams("parallel", "parallel", "arbitrary"),
        name="da_attention",
    )(lam, q, k, v, bias_tiles, out_gain)


def _mla_attn_kernel(q_ref, k_ref, v_ref, mask_ref, o_ref, m_ref, l_ref, acc_ref, *, blk, heads):
    qi = pl.program_id(2)
    m_ref[...] = jnp.full(m_ref.shape, MASK_VALUE, F32)
    l_ref[...] = jnp.zeros(l_ref.shape, F32)
    acc_ref[...] = jnp.zeros(acc_ref.shape, F32)

    def step(j, masked):
        start = pl.multiple_of(j * blk, blk)
        for h in range(heads):
            k = k_ref[0, h, pl.ds(start, blk), :]
            v = v_ref[0, h, pl.ds(start, blk), :]
            s = lax.dot_general(q_ref[0, h], k, (((1,), (1,)), ((), ())), preferred_element_type=F32)
            if masked:
                s = s + mask_ref[...]
            _softmax_step(s, v, m_ref, l_ref, acc_ref, h)

    def far_body(j, carry):
        step(j, False)
        return carry

    lax.fori_loop(0, qi, far_body, 0)
    step(qi, True)
    for h in range(heads):
        l_row = jnp.sum(l_ref[h], axis=-1, keepdims=True)
        o_ref[0, :, h * MLA_V:(h + 1) * MLA_V] = (acc_ref[h] * (1.0 / l_row)).astype(o_ref.dtype)


def mla_attention(q, k, v, mask_tile, blk, heads=2):
    b, h, s, _ = q.shape
    kern = functools.partial(_mla_attn_kernel, blk=blk, heads=heads)
    return pl.pallas_call(
        kern,
        grid=(b, h // heads, s // blk),
        in_specs=[pl.BlockSpec((1, heads, blk, MLA_QK_PAD), lambda bi, hi, i: (bi, hi, i, 0)),
                  pl.BlockSpec((1, heads, s, MLA_QK_PAD), lambda bi, hi, i: (bi, hi, 0, 0)),
                  pl.BlockSpec((1, heads, s, MLA_V), lambda bi, hi, i: (bi, hi, 0, 0)),
                  pl.BlockSpec((blk, blk), lambda bi, hi, i: (0, 0))],
        out_specs=pl.BlockSpec((1, blk, heads * MLA_V), lambda bi, hi, i: (bi, i, hi)),
        out_shape=jax.ShapeDtypeStruct((b, s, h * MLA_V), BF16),
        scratch_shapes=[pltpu.VMEM((heads, blk, STAT_LANES), F32), pltpu.VMEM((heads, blk, STAT_LANES), F32),
                        pltpu.VMEM((heads, blk, MLA_V), F32)],
        compiler_params=_cparams("parallel", "parallel", "arbitrary"),
        name="mla_attention",
    )(q, k, v, mask_tile)


def _first_max(cur, row_ids, n_rows):
    m = jnp.max(cur, axis=0, keepdims=True)
    first = jnp.min(jnp.where(cur == m, row_ids, n_rows), axis=0, keepdims=True)
    return m, row_ids == first


def _top16(s, want_rank):
    n_rows, tm = s.shape
    row_ids = lax.broadcasted_iota(jnp.int32, (n_rows, tm), 0)
    slot_ids = lax.broadcasted_iota(jnp.int32, (PEER_TOPK, tm), 0)
    cur = s
    rank = jnp.full(s.shape, float(PEER_TOPK), F32)
    vals = jnp.zeros((PEER_TOPK, tm), F32)
    for a in range(PEER_TOPK):
        m, sel = _first_max(cur, row_ids, n_rows)
        if want_rank:
            rank = jnp.where(sel, float(a), rank)
        cur = jnp.where(sel, -jnp.inf, cur)
        vals = jnp.where(slot_ids == a, m, vals)
    return vals, (rank if want_rank else cur)


def _peer_select_kernel(q_ref, keys_ref, n_ref, e1_ref, k_ref, e2_ref):
    tm = q_ref.shape[0]
    q = q_ref[...]

    def scores(c):
        return lax.dot_general(keys_ref[0, c], q[:, c * PEER_KEYS:(c + 1) * PEER_KEYS],
                               (((1,), (1,)), ((), ())), precision=lax.Precision.HIGHEST,
                               preferred_element_type=F32)

    s1, s2 = scores(0), scores(1)
    v1, rank1 = _top16(s1, True)
    v2, rank2 = _top16(s2, True)

    row8 = lax.broadcasted_iota(jnp.int32, (8, tm), 0)
    groups = [v1[0:1] + v2, v1[1:2] + v2[0:8]]
    for a in range(2, 8):
        groups.append(jnp.where(row8 < PEER_TOPK // (a + 1), v1[a:a + 1] + v2[0:8], -jnp.inf))
    groups.append(v1[8:16] + v2[0:1])
    cand = jnp.concatenate(groups, axis=0)
    n_cand = cand.shape[0]
    cand_ids = lax.broadcasted_iota(jnp.int32, (n_cand, tm), 0)
    cmax = v1[0:1] + v2[0:1]
    z = jnp.zeros((1, tm), F32)
    picked = jnp.zeros((n_cand, tm), F32)
    cur = cand
    for _ in range(PEER_TOPK):
        m, sel = _first_max(cur, cand_ids, n_cand)
        z = z + jnp.exp(m - cmax)
        picked = jnp.where(sel, 1.0, picked)
        cur = jnp.where(sel, -jnp.inf, cur)

    n_rows = [jnp.sum(picked[0:16], axis=0, keepdims=True), jnp.sum(picked[16:24], axis=0, keepdims=True)]
    for a in range(2, 8):
        lo = 24 + (a - 2) * 8
        n_rows.append(jnp.sum(picked[lo:lo + 8], axis=0, keepdims=True))
    n_sel = jnp.concatenate(n_rows + [picked[72:80]], axis=0)

    n_of_i = jnp.zeros(s1.shape, F32)
    for a in range(PEER_TOPK):
        n_of_i = jnp.where(rank1 == float(a), n_sel[a:a + 1], n_of_i)

    n_ref[0] = n_of_i
    k_ref[0] = rank2
    e1_ref[0] = jnp.exp(s1 - v1[0:1])
    e2_ref[0] = jnp.exp(s2 - v2[0:1]) / z


def peer_select(q, sub_keys, tm=512):
    t = q.shape[0]
    tm = _tile(t, tm)
    out = jax.ShapeDtypeStruct((PEER_HEADS, PEER_KEYS, t), F32)
    ospec = pl.BlockSpec((1, PEER_KEYS, tm), lambda i, h: (h, 0, i))
    return pl.pallas_call(
        _peer_select_kernel,
        grid=(t // tm, PEER_HEADS),
        in_specs=[pl.BlockSpec((tm, 2 * PEER_KEYS), lambda i, h: (i, h)),
                  pl.BlockSpec((1, 2, PEER_KEYS, PEER_KEYS), lambda i, h: (h, 0, 0, 0))],
        out_specs=[ospec, ospec, ospec, ospec],
        out_shape=[out, out, out, out],
        compiler_params=_cparams("parallel", "arbitrary"),
        name="peer_select",
    )(q, sub_keys)


def _peer_expert_kernel(ht_ref, u_ref, vt_ref, n_ref, e1_ref, k_ref, e2_ref, o_ref, acc_ref, *, n_sub):
    eb = pl.program_id(1)

    @pl.when(eb == 0)
    def _():
        acc_ref[...] = jnp.zeros(acc_ref.shape, F32)

    halves = EXPERT_CHAIN // PEER_KEYS
    total = None
    for c in range(n_sub // halves):
        rows = slice(c * EXPERT_CHAIN, (c + 1) * EXPERT_CHAIN)
        a = jnp.dot(u_ref[rows, :], ht_ref[...], preferred_element_type=F32)
        parts = []
        for half in range(halves):
            i = eb * n_sub + c * halves + half
            for jc in range(PEER_KEYS // GATE_ROWS):
                j0 = jc * GATE_ROWS
                a_rows = a[half * PEER_KEYS + j0:half * PEER_KEYS + j0 + GATE_ROWS]
                act = 0.5 * a_rows * (1.0 + lax.erf(a_rows * INV_SQRT2))
                gate = jnp.zeros(a_rows.shape, F32)
                for h in range(PEER_HEADS):
                    n_row = n_ref[h, pl.ds(i, 1), :]
                    e1_row = e1_ref[h, pl.ds(i, 1), :]
                    gate = gate + jnp.where(k_ref[h, j0:j0 + GATE_ROWS, :] < n_row,
                                            e2_ref[h, j0:j0 + GATE_ROWS, :] * e1_row, 0.0)
                parts.append((gate * act).astype(BF16))
        w = jnp.concatenate(parts, axis=0)
        part = jnp.dot(vt_ref[:, rows], w, preferred_element_type=F32)
        total = part if total is None else total + part
    acc_ref[...] += total

    @pl.when(eb == pl.num_programs(1) - 1)
    def _():
        o_ref[...] = acc_ref[...]


def peer_experts(ht, u, vt, n_i, e1, k_j, e2, tm=512, te=1024):
    d, t = ht.shape
    e = u.shape[0]
    tm, te = _tile(t, tm), _tile(e, te)
    n_sub = te // PEER_KEYS
    tspec = pl.BlockSpec((PEER_HEADS, PEER_KEYS, tm), lambda i, j: (0, 0, i))
    kern = functools.partial(_peer_expert_kernel, n_sub=n_sub)
    return pl.pallas_call(
        kern,
        grid=(t // tm, e // te),
        in_specs=[pl.BlockSpec((d, tm), lambda i, j: (0, i)),
                  pl.BlockSpec((te, d), lambda i, j: (j, 0)),
                  pl.BlockSpec((d, te), lambda i, j: (0, j)),
                  tspec, tspec, tspec, tspec],
        out_specs=pl.BlockSpec((d, tm), lambda i, j: (0, i)),
        out_shape=jax.ShapeDtypeStruct((d, t), F32),
        scratch_shapes=[pltpu.VMEM((d, tm), F32)],
        compiler_params=_cparams("parallel", "arbitrary"),
        name="peer_experts",
    )(ht, u, vt, n_i, e1, k_j, e2)


def _rms(x, g):
    return x * lax.rsqrt(jnp.mean(x * x, axis=-1, keepdims=True) + EPS) * g


def _rope(x, cos, sin):
    half = x.shape[-1] // 2
    x1, x2 = x[..., :half], x[..., half:]
    return jnp.concatenate([x1 * cos - x2 * sin, x1 * sin + x2 * cos], axis=-1)


def _t5_bucket(dist):
    n = jnp.maximum(dist, 0)
    max_exact = REL_BUCKETS // 2
    nf = jnp.maximum(n, max_exact).astype(F32)
    large = max_exact + (jnp.log(nf / max_exact) / math.log(REL_MAX_DISTANCE / max_exact)
                         * (REL_BUCKETS - max_exact)).astype(jnp.int32)
    large = jnp.minimum(large, REL_BUCKETS - 1)
    return jnp.where(n < max_exact, n, large)


def _bias_tiles(rel_bias, blk):
    assert blk >= REL_MAX_DISTANCE
    n = blk
    dist = jnp.arange(2 * n, dtype=jnp.int32)
    by_dist = jnp.take(rel_bias, _t5_bucket(dist), axis=0) - rel_bias[REL_BUCKETS - 1]
    by_dist = jnp.transpose(by_dist * LOG2E).astype(F32)
    h = by_dist.shape[0]
    desc = by_dist[:, n:0:-1]
    masked = jnp.full((h, n), MASK_VALUE, F32)
    far = jnp.zeros((h, 1), F32)
    w_diag = jnp.concatenate([by_dist[:, :1], masked, desc], axis=1)
    w_below = jnp.concatenate([desc, far, far, by_dist[:, :n:-1]], axis=1)

    def toeplitz(w):
        return jnp.tile(w, (1, n))[:, :n * 2 * n].reshape(h, n, 2 * n)[:, :, :n]

    return jnp.stack([toeplitz(w_diag), toeplitz(w_below)], axis=1)


def kernel(x, w_in, attn_norm, da_q_norm, da_k_norm, da_lambda_q1, da_lambda_k1, da_lambda_q2, da_lambda_k2, da_out_norm, da_proj, mla_q_latent_norm, mla_w_uq, mla_kv_latent_norm, mla_w_ukv, mla_q_norm, mla_k_norm, mla_proj, w_out, rel_bias, ffn_norm, peer_w_query, peer_sub_keys, peer_u, peer_v):
    b, s, d = x.shape
    t = b * s
    blk = _tile(s, 512)
    x2 = x.reshape(t, d)

    pos = jnp.arange(s, dtype=F32)
    inv_freq = ROPE_THETA ** (-jnp.arange(0, MLA_ROPE, 2, dtype=F32) / MLA_ROPE)
    ang = pos[:, None] * inv_freq[None, :]
    cos, sin = jnp.cos(ang), jnp.sin(ang)

    l = 0
    in_cols = w_in.shape[-1]
    pad_cols = (-in_cols) % 512
    w_in_p = jnp.pad(w_in[l], ((0, 0), (0, pad_cols))).astype(BF16)
    proj = norm_matmul(x2, attn_norm[l], w_in_p)
    c0 = DA_HEADS * 2 * DA_HEAD_DIM
    c1 = c0 + DA_HEADS * 2 * DA_HEAD_DIM
    c2 = c1 + DA_HEADS * DA_V_DIM
    c3 = c2 + MLA_Q_RANK
    c4 = c3 + MLA_KV_RANK
    c5 = c4 + MLA_ROPE
    qa, ka, va = proj[:, :c0], proj[:, c0:c1], proj[:, c1:c2]
    cq, ckv, kr, gates = proj[:, c2:c3], proj[:, c3:c4], proj[:, c4:c5], proj[:, c5:in_cols]

    qa = _rms(qa.reshape(b, s, DA_HEADS * 2, DA_HEAD_DIM), da_q_norm[l]) * (DA_HEAD_DIM ** -0.5 * LOG2E)
    ka = _rms(ka.reshape(b, s, DA_HEADS * 2, DA_HEAD_DIM), da_k_norm[l])
    qa = qa.reshape(b, s, c0).astype(BF16)
    ka = ka.reshape(b, s, c0).astype(BF16)
    va = va.reshape(b, s, DA_HEADS * DA_V_DIM).astype(BF16)
    lam = (jnp.exp(jnp.sum(da_lambda_q1[l] * da_lambda_k1[l]))
           - jnp.exp(jnp.sum(da_lambda_q2[l] * da_lambda_k2[l])) + LAMBDA_INIT).reshape(1).astype(F32)
    out_gain = (da_out_norm[l] * (1.0 - LAMBDA_INIT)).reshape(1, DA_V_DIM).astype(F32)
    ya = da_attention(qa, ka, va, _bias_tiles(rel_bias, blk), lam, out_gain, blk)
    ya = matmul(ya.reshape(t, -1), da_proj[l].astype(BF16))

    cqn = _rms(cq, mla_q_latent_norm[l]).astype(BF16)
    qb = matmul(cqn, mla_w_uq[l].astype(BF16), tn=MLA_HEADS * MLA_QK).reshape(b, s, MLA_HEADS, MLA_QK)
    q_rope = _rope(qb[..., MLA_NOPE:], cos[:, None, :], sin[:, None, :])
    qb = _rms(jnp.concatenate([qb[..., :MLA_NOPE], q_rope], axis=-1), mla_q_norm[l]) * (MLA_QK ** -0.5 * LOG2E)
    ckvn = _rms(ckv, mla_kv_latent_norm[l]).astype(BF16)
    kv = matmul(ckvn, mla_w_ukv[l].astype(BF16)).reshape(b, s, MLA_HEADS, MLA_NOPE + MLA_V)
    k_rope = _rope(kr.reshape(b, s, MLA_ROPE), cos, sin)
    k_rope = jnp.broadcast_to(k_rope[:, :, None, :], (b, s, MLA_HEADS, MLA_ROPE))
    kb = _rms(jnp.concatenate([kv[..., :MLA_NOPE], k_rope], axis=-1), mla_k_norm[l])
    zpad = ((0, 0), (0, 0), (0, 0), (0, MLA_QK_PAD - MLA_QK))
    qb = jnp.transpose(jnp.pad(qb, zpad), (0, 2, 1, 3)).astype(BF16)
    kb = jnp.transpose(jnp.pad(kb, zpad), (0, 2, 1, 3)).astype(BF16)
    vb = jnp.transpose(kv[..., MLA_NOPE:], (0, 2, 1, 3)).astype(BF16)
    causal = np.arange(blk)[:, None] >= np.arange(blk)[None, :]
    mask_tile = jnp.asarray(np.where(causal, 0.0, MASK_VALUE).astype(np.float32))
    yb = mla_attention(qb, kb, vb, mask_tile, blk)
    yb = matmul(yb.reshape(t, -1), mla_proj[l].astype(BF16))

    g = jax.nn.sigmoid(gates)
    merged = (g[:, :d] * ya + g[:, d:] * yb).astype(BF16)
    x1 = x2 + matmul(merged, w_out[l].astype(BF16))

    qp = norm_matmul(x1, ffn_norm[l], peer_w_query[l].astype(BF16))
    ht = jnp.transpose(_rms(x1, ffn_norm[l]).astype(BF16))
    n_i, e1, k_j, e2 = peer_select(qp, peer_sub_keys[l])
    u = peer_u[l].astype(BF16)
    vt = jnp.transpose(peer_v[l]).astype(BF16)
    out_t = peer_experts(ht, u, vt, n_i, e1, k_j, e2)
    return (x1 + jnp.transpose(out_t)).reshape(b, s, d)
```

```python
import functools
import math

import numpy as np
import jax
import jax.numpy as jnp
from jax import lax
from jax.experimental import pallas as pl
from jax.experimental.pallas import tpu as pltpu

F32 = jnp.float32
BF16 = jnp.bfloat16

EPS = 1e-6
DA_HEADS = 4
DA_HEAD_DIM = 128
DA_V_DIM = 256
MLA_HEADS = 8
MLA_NOPE = 128
MLA_ROPE = 64
MLA_QK = MLA_NOPE + MLA_ROPE
MLA_QK_PAD = 256
MLA_V = 128
MLA_Q_RANK = 384
MLA_KV_RANK = 256
ROPE_THETA = 10000.0
REL_BUCKETS = 32
REL_MAX_DISTANCE = 128
PEER_HEADS = 8
PEER_KEYS = 128
PEER_TOPK = 16
LAMBDA_INIT = 0.8 - 0.6 * math.exp(-0.3 * 0)
MASK_VALUE = -1e30
INV_SQRT2 = 0.7071067811865476
LOG2E = 1.4426950408889634
LANES = 128
SUBLANES = 8
EXPERT_CHAIN = 256
GATE_ROWS = 16

VMEM_LIMIT = 56 * 1024 * 1024


def _cparams(*sem):
    return pltpu.CompilerParams(dimension_semantics=sem, vmem_limit_bytes=VMEM_LIMIT)


def _tile(n, pref):
    t = min(n, pref)
    assert n % t == 0, (n, t)
    return t


def _norm_matmul_kernel(x_ref, g_ref, w_ref, o_ref, xn_ref):
    @pl.when(pl.program_id(1) == 0)
    def _():
        x = x_ref[...]
        ms = jnp.mean(x * x, axis=-1, keepdims=True)
        xn_ref[...] = (x * lax.rsqrt(ms + EPS) * g_ref[...]).astype(BF16)

    o_ref[...] = jnp.dot(xn_ref[...], w_ref[...], preferred_element_type=F32).astype(o_ref.dtype)


def norm_matmul(x, g, w, out_dtype=F32, tm=1024, tn=512):
    t, k = x.shape
    n = w.shape[1]
    tm, tn = _tile(t, tm), _tile(n, tn)
    return pl.pallas_call(
        _norm_matmul_kernel,
        grid=(t // tm, n // tn),
        in_specs=[pl.BlockSpec((tm, k), lambda i, j: (i, 0)),
                  pl.BlockSpec((1, k), lambda i, j: (0, 0)),
                  pl.BlockSpec((k, tn), lambda i, j: (0, j))],
        out_specs=pl.BlockSpec((tm, tn), lambda i, j: (i, j)),
        out_shape=jax.ShapeDtypeStruct((t, n), out_dtype),
        scratch_shapes=[pltpu.VMEM((tm, k), BF16)],
        compiler_params=_cparams("parallel", "arbitrary"),
        name="norm_matmul",
    )(x, g.reshape(1, k).astype(F32), w)


def _matmul_kernel(x_ref, w_ref, o_ref):
    o_ref[...] = jnp.dot(x_ref[...], w_ref[...], preferred_element_type=F32).astype(o_ref.dtype)


def matmul(x, w, out_dtype=F32, tm=1024, tn=512):
    t, k = x.shape
    n = w.shape[1]
    tm, tn = _tile(t, tm), _tile(n, tn)
    return pl.pallas_call(
        _matmul_kernel,
        grid=(t // tm, n // tn),
        in_specs=[pl.BlockSpec((tm, k), lambda i, j: (i, 0)),
                  pl.BlockSpec((k, tn), lambda i, j: (0, j))],
        out_specs=pl.BlockSpec((tm, tn), lambda i, j: (i, j)),
        out_shape=jax.ShapeDtypeStruct((t, n), out_dtype),
        compiler_params=_cparams("parallel", "arbitrary"),
        name="matmul",
    )(x, w)


def _lane_tile(x, reps):
    return x if reps == 1 else jnp.concatenate([x] * reps, axis=1)


def _softmax_step(s, v, m_ref, l_ref, acc_ref, idx):
    tk, dv = s.shape[1], v.shape[1]
    m_old = m_ref[idx]
    m_new = jnp.maximum(m_old, jnp.max(s, axis=-1, keepdims=True))
    p = jnp.exp2(s - _lane_tile(m_new, tk // LANES))
    alpha = jnp.exp2(m_old - m_new)
    p_lanes = p[:, :LANES]
    for c in range(1, tk // LANES):
        p_lanes = p_lanes + p[:, c * LANES:(c + 1) * LANES]
    l_ref[idx] = alpha * l_ref[idx] + p_lanes
    acc_ref[idx] = (_lane_tile(alpha, dv // LANES) * acc_ref[idx]
                    + jnp.dot(p.astype(BF16), v, preferred_element_type=F32))
    m_ref[idx] = m_new


def _da_attn_kernel(lam_ref, q_ref, k_ref, v_ref, bias_ref, g_ref, o_ref, m_ref, l_ref, acc_ref, *, blk):
    qi = pl.program_id(2)
    m_ref[...] = jnp.full(m_ref.shape, MASK_VALUE, F32)
    l_ref[...] = jnp.zeros(l_ref.shape, F32)
    acc_ref[...] = jnp.zeros(acc_ref.shape, F32)

    def step(j, bias_idx):
        start = pl.multiple_of(j * blk, blk)
        kblk = k_ref[0, pl.ds(start, blk), :]
        vblk = v_ref[0, pl.ds(start, blk), :]
        for mp in range(2):
            q = q_ref[0, :, mp * DA_HEAD_DIM:(mp + 1) * DA_HEAD_DIM]
            k = kblk[:, mp * DA_HEAD_DIM:(mp + 1) * DA_HEAD_DIM]
            s = lax.dot_general(q, k, (((1,), (1,)), ((), ())), preferred_element_type=F32)
            if bias_idx is not None:
                s = s + bias_ref[0, bias_idx]
            _softmax_step(s, vblk, m_ref, l_ref, acc_ref, mp)

    def far_body(j, carry):
        step(j, None)
        return carry

    lax.fori_loop(0, jnp.maximum(qi - 1, 0), far_body, 0)

    @pl.when(qi >= 1)
    def _():
        step(qi - 1, 1)

    step(qi, 0)

    o = (acc_ref[0] * (1.0 / jnp.sum(l_ref[0], axis=-1, keepdims=True))
         - acc_ref[1] * (lam_ref[0] / jnp.sum(l_ref[1], axis=-1, keepdims=True)))
    ms = jnp.mean(o * o, axis=-1, keepdims=True)
    o_ref[0] = (o * lax.rsqrt(ms + EPS) * g_ref[...]).astype(o_ref.dtype)


def da_attention(q, k, v, bias_tiles, lam, out_gain, blk):
    b, s, _ = q.shape
    hw = 2 * DA_HEAD_DIM
    kern = functools.partial(_da_attn_kernel, blk=blk)
    return pl.pallas_call(
        kern,
        grid=(b, DA_HEADS, s // blk),
        in_specs=[pl.BlockSpec(memory_space=pltpu.SMEM),
                  pl.BlockSpec((1, blk, hw), lambda bi, h, i: (bi, i, h)),
                  pl.BlockSpec((1, s, hw), lambda bi, h, i: (bi, 0, h)),
                  pl.BlockSpec((1, s, DA_V_DIM), lambda bi, h, i: (bi, 0, h)),
                  pl.BlockSpec((1, 2, blk, blk), lambda bi, h, i: (h, 0, 0, 0)),
                  pl.BlockSpec((1, DA_V_DIM), lambda bi, h, i: (0, 0))],
        out_specs=pl.BlockSpec((1, blk, DA_V_DIM), lambda bi, h, i: (bi, i, h)),
        out_shape=jax.ShapeDtypeStruct((b, s, DA_HEADS * DA_V_DIM), BF16),
        scratch_shapes=[pltpu.VMEM((2, blk, LANES), F32), pltpu.VMEM((2, blk, LANES), F32),
                        pltpu.VMEM((2, blk, DA_V_DIM), F32)],
        compiler_params=_cparams("parallel", "parallel", "arbitrary"),
        name="da_attention",
    )(lam, q, k, v, bias_tiles, out_gain)


def _mla_attn_kernel(q_ref, k_ref, v_ref, mask_ref, o_ref, m_ref, l_ref, acc_ref, *, blk, heads):
    qi = pl.program_id(2)
    m_ref[...] = jnp.full(m_ref.shape, MASK_VALUE, F32)
    l_ref[...] = jnp.zeros(l_ref.shape, F32)
    acc_ref[...] = jnp.zeros(acc_ref.shape, F32)

    def step(j, masked):
        start = pl.multiple_of(j * blk, blk)
        for h in range(heads):
            k = k_ref[0, h, pl.ds(start, blk), :]
            v = v_ref[0, h, pl.ds(start, blk), :]
            s = lax.dot_general(q_ref[0, h], k, (((1,), (1,)), ((), ())), preferred_element_type=F32)
            if masked:
                s = s + mask_ref[...]
            _softmax_step(s, v, m_ref, l_ref, acc_ref, h)

    def far_body(j, carry):
        step(j, False)
        return carry

    lax.fori_loop(0, qi, far_body, 0)
    step(qi, True)
    for h in range(heads):
        l_row = jnp.sum(l_ref[h], axis=-1, keepdims=True)
        o_ref[0, :, h * MLA_V:(h + 1) * MLA_V] = (acc_ref[h] * (1.0 / l_row)).astype(o_ref.dtype)


def mla_attention(q, k, v, mask_tile, blk, heads=2):
    b, h, s, _ = q.shape
    kern = functools.partial(_mla_attn_kernel, blk=blk, heads=heads)
    return pl.pallas_call(
        kern,
        grid=(b, h // heads, s // blk),
        in_specs=[pl.BlockSpec((1, heads, blk, MLA_QK_PAD), lambda bi, hi, i: (bi, hi, i, 0)),
                  pl.BlockSpec((1, heads, s, MLA_QK_PAD), lambda bi, hi, i: (bi, hi, 0, 0)),
                  pl.BlockSpec((1, heads, s, MLA_V), lambda bi, hi, i: (bi, hi, 0, 0)),
                  pl.BlockSpec((blk, blk), lambda bi, hi, i: (0, 0))],
        out_specs=pl.BlockSpec((1, blk, heads * MLA_V), lambda bi, hi, i: (bi, i, hi)),
        out_shape=jax.ShapeDtypeStruct((b, s, h * MLA_V), BF16),
        scratch_shapes=[pltpu.VMEM((heads, blk, LANES), F32), pltpu.VMEM((heads, blk, LANES), F32),
                        pltpu.VMEM((heads, blk, MLA_V), F32)],
        compiler_params=_cparams("parallel", "parallel", "arbitrary"),
        name="mla_attention",
    )(q, k, v, mask_tile)


def _first_max(cur, row_ids, n_rows):
    m = jnp.max(cur, axis=0, keepdims=True)
    first = jnp.min(jnp.where(cur == m, row_ids, n_rows), axis=0, keepdims=True)
    return m, row_ids == first


def _top16(s, want_rank):
    n_rows, tm = s.shape
    row_ids = lax.broadcasted_iota(jnp.int32, (n_rows, tm), 0)
    slot_ids = lax.broadcasted_iota(jnp.int32, (PEER_TOPK, tm), 0)
    cur = s
    rank = jnp.full(s.shape, float(PEER_TOPK), F32)
    vals = jnp.zeros((PEER_TOPK, tm), F32)
    for a in range(PEER_TOPK):
        m, sel = _first_max(cur, row_ids, n_rows)
        if want_rank:
            rank = jnp.where(sel, float(a), rank)
        cur = jnp.where(sel, -jnp.inf, cur)
        vals = jnp.where(slot_ids == a, m, vals)
    return vals, (rank if want_rank else cur)


def _peer_select_kernel(q_ref, keys_ref, n_ref, e1_ref, k_ref, e2_ref):
    tm = q_ref.shape[0]
    q = q_ref[...]

    def scores(c):
        return lax.dot_general(keys_ref[0, c], q[:, c * PEER_KEYS:(c + 1) * PEER_KEYS],
                               (((1,), (1,)), ((), ())), precision=lax.Precision.HIGHEST,
                               preferred_element_type=F32)

    s1, s2 = scores(0), scores(1)
    v1, rank1 = _top16(s1, True)
    v2, rank2 = _top16(s2, True)

    row8 = lax.broadcasted_iota(jnp.int32, (8, tm), 0)
    groups = [v1[0:1] + v2, v1[1:2] + v2[0:8]]
    for a in range(2, 8):
        groups.append(jnp.where(row8 < PEER_TOPK // (a + 1), v1[a:a + 1] + v2[0:8], -jnp.inf))
    groups.append(v1[8:16] + v2[0:1])
    cand = jnp.concatenate(groups, axis=0)
    n_cand = cand.shape[0]
    cand_ids = lax.broadcasted_iota(jnp.int32, (n_cand, tm), 0)
    cmax = v1[0:1] + v2[0:1]
    z = jnp.zeros((1, tm), F32)
    picked = jnp.zeros((n_cand, tm), F32)
    cur = cand
    for _ in range(PEER_TOPK):
        m, sel = _first_max(cur, cand_ids, n_cand)
        z = z + jnp.exp(m - cmax)
        picked = jnp.where(sel, 1.0, picked)
        cur = jnp.where(sel, -jnp.inf, cur)

    n_rows = [jnp.sum(picked[0:16], axis=0, keepdims=True), jnp.sum(picked[16:24], axis=0, keepdims=True)]
    for a in range(2, 8):
        lo = 24 + (a - 2) * 8
        n_rows.append(jnp.sum(picked[lo:lo + 8], axis=0, keepdims=True))
    n_sel = jnp.concatenate(n_rows + [picked[72:80]], axis=0)

    n_of_i = jnp.zeros(s1.shape, F32)
    for a in range(PEER_TOPK):
        n_of_i = jnp.where(rank1 == float(a), n_sel[a:a + 1], n_of_i)

    e1 = jnp.exp(s1 - v1[0:1])
    for lt in range(tm // LANES):
        lanes = slice(lt * LANES, (lt + 1) * LANES)
        n_ref[0, lt] = n_of_i[:, lanes]
        e1_ref[0, lt] = e1[:, lanes]
    k_ref[0] = rank2
    e2_ref[0] = jnp.exp(s2 - v2[0:1]) / z


def peer_select(q, sub_keys, tm=512):
    t = q.shape[0]
    tm = _tile(t, tm)
    out_i = jax.ShapeDtypeStruct((PEER_HEADS, t // LANES, PEER_KEYS, LANES), F32)
    out_j = jax.ShapeDtypeStruct((PEER_HEADS, PEER_KEYS, t), F32)
    ispec = pl.BlockSpec((1, tm // LANES, PEER_KEYS, LANES), lambda i, h: (h, i, 0, 0))
    jspec = pl.BlockSpec((1, PEER_KEYS, tm), lambda i, h: (h, 0, i))
    return pl.pallas_call(
        _peer_select_kernel,
        grid=(t // tm, PEER_HEADS),
        in_specs=[pl.BlockSpec((tm, 2 * PEER_KEYS), lambda i, h: (i, h)),
                  pl.BlockSpec((1, 2, PEER_KEYS, PEER_KEYS), lambda i, h: (h, 0, 0, 0))],
        out_specs=[ispec, ispec, jspec, jspec],
        out_shape=[out_i, out_i, out_j, out_j],
        compiler_params=_cparams("parallel", "arbitrary"),
        name="peer_select",
    )(q, sub_keys)


def _peer_expert_kernel(ht_ref, u_ref, vt_ref, n_ref, e1_ref, k_ref, e2_ref, o_ref,
                        acc_ref, a_buf, w_cur, w_prev, *, n_chain, tm):
    eb = pl.program_id(1)
    n_blocks = pl.num_programs(1) - 1
    halves = EXPERT_CHAIN // PEER_KEYS

    def chain_ds(c):
        return pl.ds(pl.multiple_of(c * EXPERT_CHAIN, EXPERT_CHAIN), EXPERT_CHAIN)

    def stage_a(c, slot):
        a_buf[slot] = jnp.dot(u_ref[chain_ds(c), :], ht_ref[...], preferred_element_type=F32)

    def stage_b(c, slot):
        for half in range(halves):
            i = (eb * n_chain + c) * halves + half
            row_i = pl.ds(i, 1)
            tile = (SUBLANES, LANES)
            for lt in range(tm // LANES):
                lanes = slice(lt * LANES, (lt + 1) * LANES)
                n_rows = [jnp.broadcast_to(n_ref[h, lt, row_i, :], tile) for h in range(PEER_HEADS)]
                e1_rows = [jnp.broadcast_to(e1_ref[h, lt, row_i, :], tile) for h in range(PEER_HEADS)]
                for jc in range(PEER_KEYS // GATE_ROWS):
                    parts = []
                    for sub in range(GATE_ROWS // SUBLANES):
                        j0 = jc * GATE_ROWS + sub * SUBLANES
                        a_rows = a_buf[slot, half * PEER_KEYS + j0:half * PEER_KEYS + j0 + SUBLANES, lanes]
                        act = 0.5 * a_rows * (1.0 + lax.erf(a_rows * INV_SQRT2))
                        gate = jnp.zeros(a_rows.shape, F32)
                        for h in range(PEER_HEADS):
                            gate = gate + jnp.where(k_ref[h, j0:j0 + SUBLANES, lanes] < n_rows[h],
                                                    e2_ref[h, j0:j0 + SUBLANES, lanes] * e1_rows[h], 0.0)
                        parts.append(gate * act)
                    r0 = half * PEER_KEYS + jc * GATE_ROWS
                    w_cur[c, r0:r0 + GATE_ROWS, lanes] = jnp.concatenate(parts, axis=0).astype(BF16)

    def stage_c(c):
        acc_ref[...] += jnp.dot(vt_ref[:, chain_ds(c)], w_prev[c], preferred_element_type=F32)

    @pl.when(eb == 0)
    def _():
        acc_ref[...] = jnp.zeros(acc_ref.shape, F32)
        w_prev[...] = jnp.zeros(w_prev.shape, BF16)

    @pl.when(eb < n_blocks)
    def _():
        stage_a(0, 0)

        def body(pair, carry):
            c = 2 * pair
            stage_a(c + 1, 1)
            stage_b(c, 0)
            stage_c(c)
            stage_a(jnp.minimum(c + 2, n_chain - 1), 0)
            stage_b(c + 1, 1)
            stage_c(c + 1)
            return carry

        lax.fori_loop(0, n_chain // 2, body, 0)
        w_prev[...] = w_cur[...]

    @pl.when(eb == n_blocks)
    def _():
        def drain(c, carry):
            stage_c(c)
            return carry

        lax.fori_loop(0, n_chain, drain, 0)
        o_ref[...] = acc_ref[...]


def peer_experts(ht, u, vt, n_i, e1, k_j, e2, tm=512, te=2048):
    d, t = ht.shape
    e = u.shape[0]
    tm, te = _tile(t, tm), _tile(e, te)
    n_chain = te // EXPERT_CHAIN
    n_blocks = e // te
    assert n_chain % 2 == 0
    ispec = pl.BlockSpec((PEER_HEADS, tm // LANES, PEER_KEYS, LANES), lambda i, j: (0, i, 0, 0))
    jspec = pl.BlockSpec((PEER_HEADS, PEER_KEYS, tm), lambda i, j: (0, 0, i))
    kern = functools.partial(_peer_expert_kernel, n_chain=n_chain, tm=tm)
    return pl.pallas_call(
        kern,
        grid=(t // tm, n_blocks + 1),
        in_specs=[pl.BlockSpec((d, tm), lambda i, j: (0, i)),
                  pl.BlockSpec((te, d), lambda i, j: (jnp.minimum(j, n_blocks - 1), 0)),
                  pl.BlockSpec((d, te), lambda i, j: (0, jnp.maximum(j - 1, 0))),
                  ispec, ispec, jspec, jspec],
        out_specs=pl.BlockSpec((d, tm), lambda i, j: (0, i)),
        out_shape=jax.ShapeDtypeStruct((d, t), F32),
        scratch_shapes=[pltpu.VMEM((d, tm), F32),
                        pltpu.VMEM((2, EXPERT_CHAIN, tm), F32),
                        pltpu.VMEM((n_chain, EXPERT_CHAIN, tm), BF16),
                        pltpu.VMEM((n_chain, EXPERT_CHAIN, tm), BF16)],
        compiler_params=_cparams("parallel", "arbitrary"),
        name="peer_experts",
    )(ht, u, vt, n_i, e1, k_j, e2)


def _rms(x, g):
    return x * lax.rsqrt(jnp.mean(x * x, axis=-1, keepdims=True) + EPS) * g


def _rope(x, cos, sin):
    half = x.shape[-1] // 2
    x1, x2 = x[..., :half], x[..., half:]
    return jnp.concatenate([x1 * cos - x2 * sin, x1 * sin + x2 * cos], axis=-1)


def _t5_bucket(dist):
    n = jnp.maximum(dist, 0)
    max_exact = REL_BUCKETS // 2
    nf = jnp.maximum(n, max_exact).astype(F32)
    large = max_exact + (jnp.log(nf / max_exact) / math.log(REL_MAX_DISTANCE / max_exact)
                         * (REL_BUCKETS - max_exact)).astype(jnp.int32)
    large = jnp.minimum(large, REL_BUCKETS - 1)
    return jnp.where(n < max_exact, n, large)


def _bias_tiles(rel_bias, blk):
    assert blk >= REL_MAX_DISTANCE
    n = blk
    dist = jnp.arange(2 * n, dtype=jnp.int32)
    by_dist = jnp.take(rel_bias, _t5_bucket(dist), axis=0) - rel_bias[REL_BUCKETS - 1]
    by_dist = jnp.transpose(by_dist * LOG2E).astype(F32)
    h = by_dist.shape[0]
    desc = by_dist[:, n:0:-1]
    masked = jnp.full((h, n), MASK_VALUE, F32)
    far = jnp.zeros((h, 1), F32)
    w_diag = jnp.concatenate([by_dist[:, :1], masked, desc], axis=1)
    w_below = jnp.concatenate([desc, far, far, by_dist[:, :n:-1]], axis=1)

    def toeplitz(w):
        return jnp.tile(w, (1, n))[:, :n * 2 * n].reshape(h, n, 2 * n)[:, :, :n]

    return jnp.stack([toeplitz(w_diag), toeplitz(w_below)], axis=1)


def kernel(x, w_in, attn_norm, da_q_norm, da_k_norm, da_lambda_q1, da_lambda_k1, da_lambda_q2, da_lambda_k2, da_out_norm, da_proj, mla_q_latent_norm, mla_w_uq, mla_kv_latent_norm, mla_w_ukv, mla_q_norm, mla_k_norm, mla_proj, w_out, rel_bias, ffn_norm, peer_w_query, peer_sub_keys, peer_u, peer_v):
    b, s, d = x.shape
    t = b * s
    blk = _tile(s, 512)
    x2 = x.reshape(t, d)

    pos = jnp.arange(s, dtype=F32)
    inv_freq = ROPE_THETA ** (-jnp.arange(0, MLA_ROPE, 2, dtype=F32) / MLA_ROPE)
    ang = pos[:, None] * inv_freq[None, :]
    cos, sin = jnp.cos(ang), jnp.sin(ang)

    l = 0
    in_cols = w_in.shape[-1]
    pad_cols = (-in_cols) % 512
    w_in_p = jnp.pad(w_in[l], ((0, 0), (0, pad_cols))).astype(BF16)
    proj = norm_matmul(x2, attn_norm[l], w_in_p)
    c0 = DA_HEADS * 2 * DA_HEAD_DIM
    c1 = c0 + DA_HEADS * 2 * DA_HEAD_DIM
    c2 = c1 + DA_HEADS * DA_V_DIM
    c3 = c2 + MLA_Q_RANK
    c4 = c3 + MLA_KV_RANK
    c5 = c4 + MLA_ROPE
    qa, ka, va = proj[:, :c0], proj[:, c0:c1], proj[:, c1:c2]
    cq, ckv, kr, gates = proj[:, c2:c3], proj[:, c3:c4], proj[:, c4:c5], proj[:, c5:in_cols]

    qa = _rms(qa.reshape(b, s, DA_HEADS * 2, DA_HEAD_DIM), da_q_norm[l]) * (DA_HEAD_DIM ** -0.5 * LOG2E)
    ka = _rms(ka.reshape(b, s, DA_HEADS * 2, DA_HEAD_DIM), da_k_norm[l])
    qa = qa.reshape(b, s, c0).astype(BF16)
    ka = ka.reshape(b, s, c0).astype(BF16)
    va = va.reshape(b, s, DA_HEADS * DA_V_DIM).astype(BF16)
    lam = (jnp.exp(jnp.sum(da_lambda_q1[l] * da_lambda_k1[l]))
           - jnp.exp(jnp.sum(da_lambda_q2[l] * da_lambda_k2[l])) + LAMBDA_INIT).reshape(1).astype(F32)
    out_gain = (da_out_norm[l] * (1.0 - LAMBDA_INIT)).reshape(1, DA_V_DIM).astype(F32)
    ya = da_attention(qa, ka, va, _bias_tiles(rel_bias, blk), lam, out_gain, blk)
    ya = matmul(ya.reshape(t, -1), da_proj[l].astype(BF16))

    cqn = _rms(cq, mla_q_latent_norm[l]).astype(BF16)
    qb = matmul(cqn, mla_w_uq[l].astype(BF16), tn=MLA_HEADS * MLA_QK).reshape(b, s, MLA_HEADS, MLA_QK)
    q_rope = _rope(qb[..., MLA_NOPE:], cos[:, None, :], sin[:, None, :])
    qb = _rms(jnp.concatenate([qb[..., :MLA_NOPE], q_rope], axis=-1), mla_q_norm[l]) * (MLA_QK ** -0.5 * LOG2E)
    ckvn = _rms(ckv, mla_kv_latent_norm[l]).astype(BF16)
    kv = matmul(ckvn, mla_w_ukv[l].astype(BF16)).reshape(b, s, MLA_HEADS, MLA_NOPE + MLA_V)
    k_rope = _rope(kr.reshape(b, s, MLA_ROPE), cos, sin)
    k_rope = jnp.broadcast_to(k_rope[:, :, None, :], (b, s, MLA_HEADS, MLA_ROPE))
    kb = _rms(jnp.concatenate([kv[..., :MLA_NOPE], k_rope], axis=-1), mla_k_norm[l])
    zpad = ((0, 0), (0, 0), (0, 0), (0, MLA_QK_PAD - MLA_QK))
    qb = jnp.transpose(jnp.pad(qb, zpad), (0, 2, 1, 3)).astype(BF16)
    kb = jnp.transpose(jnp.pad(kb, zpad), (0, 2, 1, 3)).astype(BF16)
    vb = jnp.transpose(kv[..., MLA_NOPE:], (0, 2, 1, 3)).astype(BF16)
    causal = np.arange(blk)[:, None] >= np.arange(blk)[None, :]
    mask_tile = jnp.asarray(np.where(causal, 0.0, MASK_VALUE).astype(np.float32))
    yb = mla_attention(qb, kb, vb, mask_tile, blk)
    yb = matmul(yb.reshape(t, -1), mla_proj[l].astype(BF16))

    g = jax.nn.sigmoid(gates)
    merged = (g[:, :d] * ya + g[:, d:] * yb).astype(BF16)
    x1 = x2 + matmul(merged, w_out[l].astype(BF16))

    qp = norm_matmul(x1, ffn_norm[l], peer_w_query[l].astype(BF16))
    ht = jnp.transpose(_rms(x1, ffn_norm[l]).astype(BF16))
    n_i, e1, k_j, e2 = peer_select(qp, peer_sub_keys[l])
    u = peer_u[l].astype(BF16)
    vt = jnp.transpose(peer_v[l]).astype(BF16)
    out_t = peer_experts(ht, u, vt, n_i, e1, k_j, e2)
    return (x1 + jnp.transpose(out_t)).reshape(b, s, d)
```

```python
import functools
import math

import numpy as np
import jax
import jax.numpy as jnp
from jax import lax
from jax.experimental import pallas as pl
from jax.experimental.pallas import tpu as pltpu

F32 = jnp.float32
BF16 = jnp.bfloat16

EPS = 1e-6
DA_HEADS = 4
DA_HEAD_DIM = 128
DA_V_DIM = 256
MLA_HEADS = 8
MLA_NOPE = 128
MLA_ROPE = 64
MLA_QK = MLA_NOPE + MLA_ROPE
MLA_QK_PAD = 256
MLA_V = 128
MLA_Q_RANK = 384
MLA_KV_RANK = 256
ROPE_THETA = 10000.0
REL_BUCKETS = 32
REL_MAX_DISTANCE = 128
PEER_HEADS = 8
PEER_KEYS = 128
PEER_TOPK = 16
LAMBDA_INIT = 0.8 - 0.6 * math.exp(-0.3 * 0)
MASK_VALUE = -1e30
INV_SQRT2 = 0.7071067811865476
LOG2E = 1.4426950408889634
LANES = 128
SUBLANES = 8
EXPERT_CHAIN = 256
GATE_ROWS = 16

VMEM_LIMIT = 56 * 1024 * 1024


def _cparams(*sem):
    return pltpu.CompilerParams(dimension_semantics=sem, vmem_limit_bytes=VMEM_LIMIT)


def _tile(n, pref):
    t = min(n, pref)
    assert n % t == 0, (n, t)
    return t


def _rms_rows(x, gain_row, width=None):
    ms = jnp.sum(x * x, axis=-1, keepdims=True) * (1.0 / (width or x.shape[-1]))
    return x * lax.rsqrt(ms + EPS) * gain_row


def _in_proj_kernel(x_ref, g_ref, w_ref, o_ref, xn_ref, *, sigmoid):
    @pl.when(pl.program_id(1) == 0)
    def _():
        xn_ref[...] = _rms_rows(x_ref[...], g_ref[...]).astype(BF16)

    y = jnp.dot(xn_ref[...], w_ref[...], preferred_element_type=F32)
    if sigmoid:
        y = 1.0 / (1.0 + jnp.exp(-y))
    o_ref[...] = y.astype(o_ref.dtype)


def _in_proj_head_norm_kernel(x_ref, g_ref, w_ref, cg_ref, o_ref, xn_ref):
    @pl.when(pl.program_id(1) == 0)
    def _():
        xn_ref[...] = _rms_rows(x_ref[...], g_ref[...]).astype(BF16)

    y = jnp.dot(xn_ref[...], w_ref[...], preferred_element_type=F32)
    for c in range(y.shape[1] // LANES):
        cols = slice(c * LANES, (c + 1) * LANES)
        o_ref[:, cols] = _rms_rows(y[:, cols], cg_ref[:, cols]).astype(o_ref.dtype)


def in_proj(x, g, w, col_gain=None, sigmoid=False, tm=1024, tn=512):
    t, k = x.shape
    n = w.shape[1]
    tm, tn = _tile(t, tm), _tile(n, tn)
    in_specs = [pl.BlockSpec((tm, k), lambda i, j: (i, 0)),
                pl.BlockSpec((1, k), lambda i, j: (0, 0)),
                pl.BlockSpec((k, tn), lambda i, j: (0, j))]
    args = [x, g.reshape(1, k).astype(F32), w]
    if col_gain is None:
        kern = functools.partial(_in_proj_kernel, sigmoid=sigmoid)
    else:
        assert not sigmoid
        kern = _in_proj_head_norm_kernel
        in_specs.append(pl.BlockSpec((1, tn), lambda i, j: (0, j)))
        args.append(col_gain.reshape(1, n).astype(F32))
    return pl.pallas_call(
        kern,
        grid=(t // tm, n // tn),
        in_specs=in_specs,
        out_specs=pl.BlockSpec((tm, tn), lambda i, j: (i, j)),
        out_shape=jax.ShapeDtypeStruct((t, n), BF16),
        scratch_shapes=[pltpu.VMEM((tm, k), BF16)],
        compiler_params=_cparams("parallel", "arbitrary"),
        name="in_proj",
    )(*args)


def _latent_kernel(x_ref, g_ref, w_ref, gq_ref, gkv_ref, cos_ref, sin_ref, cq_ref, ckv_ref, kr_ref):
    xn = _rms_rows(x_ref[...], g_ref[...]).astype(BF16)
    y = jnp.dot(xn, w_ref[...], preferred_element_type=F32)
    c0, c1 = MLA_Q_RANK, MLA_Q_RANK + MLA_KV_RANK
    cq_ref[...] = _rms_rows(y[:, :c0], gq_ref[...]).astype(BF16)
    ckv_ref[...] = _rms_rows(y[:, c0:c1], gkv_ref[...]).astype(BF16)
    kr_ref[...] = y[:, c1:c1 + LANES] * cos_ref[...] + y[:, c1 + LANES:] * sin_ref[...]


def latent_proj(x, g, w, gq, gkv, cos_t, sin_t, seq, tm=1024):
    t, k = x.shape
    tm = _tile(seq, tm)
    spb = seq // tm
    n = w.shape[1]
    row = lambda a: a.reshape(1, -1).astype(F32)
    return pl.pallas_call(
        _latent_kernel,
        grid=(t // tm,),
        in_specs=[pl.BlockSpec((tm, k), lambda i: (i, 0)),
                  pl.BlockSpec((1, k), lambda i: (0, 0)),
                  pl.BlockSpec((k, n), lambda i: (0, 0)),
                  pl.BlockSpec((1, MLA_Q_RANK), lambda i: (0, 0)),
                  pl.BlockSpec((1, MLA_KV_RANK), lambda i: (0, 0)),
                  pl.BlockSpec((tm, LANES), lambda i: (i % spb, 0)),
                  pl.BlockSpec((tm, LANES), lambda i: (i % spb, 0))],
        out_specs=[pl.BlockSpec((tm, MLA_Q_RANK), lambda i: (i, 0)),
                   pl.BlockSpec((tm, MLA_KV_RANK), lambda i: (i, 0)),
                   pl.BlockSpec((tm, LANES), lambda i: (i, 0))],
        out_shape=[jax.ShapeDtypeStruct((t, MLA_Q_RANK), BF16),
                   jax.ShapeDtypeStruct((t, MLA_KV_RANK), BF16),
                   jax.ShapeDtypeStruct((t, LANES), F32)],
        compiler_params=_cparams("parallel"),
        name="latent_proj",
    )(x, row(g), w, row(gq), row(gkv), cos_t, sin_t)


def _mla_q_kernel(cq_ref, w_ref, cos_ref, sin_ref, g_ref, o_ref):
    y = jnp.dot(cq_ref[...], w_ref[0], preferred_element_type=F32)
    nope = y[:, :LANES]
    rope = y[:, LANES:2 * LANES] * cos_ref[...] + y[:, 2 * LANES:] * sin_ref[...]
    ms = (jnp.sum(nope * nope, axis=-1, keepdims=True)
          + jnp.sum(rope * rope, axis=-1, keepdims=True)) * (1.0 / MLA_QK)
    sc = lax.rsqrt(ms + EPS)
    o_ref[0, 0, :, :LANES] = (nope * sc * g_ref[:, :LANES]).astype(BF16)
    o_ref[0, 0, :, LANES:] = (rope * sc * g_ref[:, LANES:]).astype(BF16)


def mla_q_proj(cq, w, cos_t, sin_t, gain, batch, seq, tm=1024):
    tm = _tile(seq, tm)
    spb = seq // tm
    h, r, n = w.shape
    return pl.pallas_call(
        _mla_q_kernel,
        grid=(batch * spb, h),
        in_specs=[pl.BlockSpec((tm, r), lambda i, hi: (i, 0)),
                  pl.BlockSpec((1, r, n), lambda i, hi: (hi, 0, 0)),
                  pl.BlockSpec((tm, LANES), lambda i, hi: (i % spb, 0)),
                  pl.BlockSpec((tm, LANES), lambda i, hi: (i % spb, 0)),
                  pl.BlockSpec((1, MLA_QK_PAD), lambda i, hi: (0, 0))],
        out_specs=pl.BlockSpec((1, 1, tm, MLA_QK_PAD), lambda i, hi: (i // spb, hi, i % spb, 0)),
        out_shape=jax.ShapeDtypeStruct((batch, h, seq, MLA_QK_PAD), BF16),
        compiler_params=_cparams("parallel", "arbitrary"),
        name="mla_q_proj",
    )(cq, w, cos_t, sin_t, gain)


def _mla_kv_kernel(ckv_ref, w_ref, kr_ref, g_ref, k_ref, v_ref):
    y = jnp.dot(ckv_ref[...], w_ref[0], preferred_element_type=F32)
    nope = y[:, :LANES]
    kr = kr_ref[...]
    ms = (jnp.sum(nope * nope, axis=-1, keepdims=True)
          + jnp.sum(kr * kr, axis=-1, keepdims=True)) * (1.0 / MLA_QK)
    sc = lax.rsqrt(ms + EPS)
    k_ref[0, 0, :, :LANES] = (nope * sc * g_ref[:, :LANES]).astype(BF16)
    k_ref[0, 0, :, LANES:] = (kr * sc * g_ref[:, LANES:]).astype(BF16)
    v_ref[0, 0] = y[:, LANES:].astype(BF16)


def mla_kv_proj(ckv, w, kr, gain, batch, seq, tm=1024):
    tm = _tile(seq, tm)
    spb = seq // tm
    h, r, n = w.shape
    return pl.pallas_call(
        _mla_kv_kernel,
        grid=(batch * spb, h),
        in_specs=[pl.BlockSpec((tm, r), lambda i, hi: (i, 0)),
                  pl.BlockSpec((1, r, n), lambda i, hi: (hi, 0, 0)),
                  pl.BlockSpec((tm, LANES), lambda i, hi: (i, 0)),
                  pl.BlockSpec((1, MLA_QK_PAD), lambda i, hi: (0, 0))],
        out_specs=[pl.BlockSpec((1, 1, tm, MLA_QK_PAD), lambda i, hi: (i // spb, hi, i % spb, 0)),
                   pl.BlockSpec((1, 1, tm, MLA_V), lambda i, hi: (i // spb, hi, i % spb, 0))],
        out_shape=[jax.ShapeDtypeStruct((batch, h, seq, MLA_QK_PAD), BF16),
                   jax.ShapeDtypeStruct((batch, h, seq, MLA_V), BF16)],
        compiler_params=_cparams("parallel", "arbitrary"),
        name="mla_kv_proj",
    )(ckv, w, kr, gain)


def _merge_kernel(ya_ref, yb_ref, gate_ref, x_ref, wa_ref, wb_ref, wo_ref, fg_ref, wq_ref,
                  x1_ref, ht_ref, qp_ref):
    d = x_ref.shape[1]
    ya = jnp.dot(ya_ref[...], wa_ref[...], preferred_element_type=F32)
    yb = jnp.dot(yb_ref[...], wb_ref[...], preferred_element_type=F32)
    merged = gate_ref[:, :d].astype(F32) * ya + gate_ref[:, d:].astype(F32) * yb
    x1 = x_ref[...] + jnp.dot(merged.astype(BF16), wo_ref[...], preferred_element_type=F32)
    x1_ref[...] = x1
    h2 = _rms_rows(x1, fg_ref[...])
    ht_ref[...] = jnp.transpose(h2).astype(BF16)
    qp_ref[...] = jnp.dot(h2.astype(BF16), wq_ref[...], preferred_element_type=F32)


def merge_and_query(ya, yb, gates, x, w_da, w_mla, w_out, ffn_gain, w_query, tm=256):
    t, d = x.shape
    tm = _tile(t, tm)
    nq = w_query.shape[1]
    full = lambda a: pl.BlockSpec(a.shape, lambda i: (0, 0))
    rows = lambda n: pl.BlockSpec((tm, n), lambda i: (i, 0))
    fg = ffn_gain.reshape(1, d).astype(F32)
    return pl.pallas_call(
        _merge_kernel,
        grid=(t // tm,),
        in_specs=[rows(ya.shape[1]), rows(yb.shape[1]), rows(gates.shape[1]), rows(d),
                  full(w_da), full(w_mla), full(w_out), full(fg), full(w_query)],
        out_specs=[rows(d), pl.BlockSpec((d, tm), lambda i: (0, i)), rows(nq)],
        out_shape=[jax.ShapeDtypeStruct((t, d), F32), jax.ShapeDtypeStruct((d, t), BF16),
                   jax.ShapeDtypeStruct((t, nq), F32)],
        compiler_params=_cparams("parallel"),
        name="merge_and_query",
    )(ya, yb, gates, x, w_da, w_mla, w_out, fg, w_query)


def _lane_tile(x, reps):
    return x if reps == 1 else jnp.concatenate([x] * reps, axis=1)


def _softmax_step(s, v, m_ref, l_ref, acc_ref, idx):
    tk, dv = s.shape[1], v.shape[1]
    m_old = m_ref[idx]
    m_new = jnp.maximum(m_old, jnp.max(s, axis=-1, keepdims=True))
    p = jnp.exp2(s - _lane_tile(m_new, tk // LANES))
    alpha = jnp.exp2(m_old - m_new)
    p_lanes = p[:, :LANES]
    for c in range(1, tk // LANES):
        p_lanes = p_lanes + p[:, c * LANES:(c + 1) * LANES]
    l_ref[idx] = alpha * l_ref[idx] + p_lanes
    acc_ref[idx] = (_lane_tile(alpha, dv // LANES) * acc_ref[idx]
                    + jnp.dot(p.astype(BF16), v, preferred_element_type=F32))
    m_ref[idx] = m_new


def _da_attn_kernel(lam_ref, q_ref, k_ref, v_ref, bias_ref, g_ref, o_ref, m_ref, l_ref, acc_ref, *, blk):
    qi = pl.program_id(2)
    m_ref[...] = jnp.full(m_ref.shape, MASK_VALUE, F32)
    l_ref[...] = jnp.zeros(l_ref.shape, F32)
    acc_ref[...] = jnp.zeros(acc_ref.shape, F32)

    def step(j, bias_idx):
        start = pl.multiple_of(j * blk, blk)
        kblk = k_ref[0, pl.ds(start, blk), :]
        vblk = v_ref[0, pl.ds(start, blk), :]
        for mp in range(2):
            q = q_ref[0, :, mp * DA_HEAD_DIM:(mp + 1) * DA_HEAD_DIM]
            k = kblk[:, mp * DA_HEAD_DIM:(mp + 1) * DA_HEAD_DIM]
            s = lax.dot_general(q, k, (((1,), (1,)), ((), ())), preferred_element_type=F32)
            if bias_idx is not None:
                s = s + bias_ref[0, bias_idx]
            _softmax_step(s, vblk, m_ref, l_ref, acc_ref, mp)

    def far_body(j, carry):
        step(j, None)
        return carry

    lax.fori_loop(0, jnp.maximum(qi - 1, 0), far_body, 0)

    @pl.when(qi >= 1)
    def _():
        step(qi - 1, 1)

    step(qi, 0)

    o = (acc_ref[0] * (1.0 / jnp.sum(l_ref[0], axis=-1, keepdims=True))
         - acc_ref[1] * (lam_ref[0] / jnp.sum(l_ref[1], axis=-1, keepdims=True)))
    ms = jnp.mean(o * o, axis=-1, keepdims=True)
    o_ref[0] = (o * lax.rsqrt(ms + EPS) * g_ref[...]).astype(o_ref.dtype)


def da_attention(qk, v, bias_tiles, lam, out_gain, blk):
    b, s, _ = qk.shape
    hw = 2 * DA_HEAD_DIM
    kern = functools.partial(_da_attn_kernel, blk=blk)
    return pl.pallas_call(
        kern,
        grid=(b, DA_HEADS, s // blk),
        in_specs=[pl.BlockSpec(memory_space=pltpu.SMEM),
                  pl.BlockSpec((1, blk, hw), lambda bi, h, i: (bi, i, h)),
                  pl.BlockSpec((1, s, hw), lambda bi, h, i: (bi, 0, DA_HEADS + h)),
                  pl.BlockSpec((1, s, DA_V_DIM), lambda bi, h, i: (bi, 0, h)),
                  pl.BlockSpec((1, 2, blk, blk), lambda bi, h, i: (h, 0, 0, 0)),
                  pl.BlockSpec((1, DA_V_DIM), lambda bi, h, i: (0, 0))],
        out_specs=pl.BlockSpec((1, blk, DA_V_DIM), lambda bi, h, i: (bi, i, h)),
        out_shape=jax.ShapeDtypeStruct((b, s, DA_HEADS * DA_V_DIM), BF16),
        scratch_shapes=[pltpu.VMEM((2, blk, LANES), F32), pltpu.VMEM((2, blk, LANES), F32),
                        pltpu.VMEM((2, blk, DA_V_DIM), F32)],
        compiler_params=_cparams("parallel", "parallel", "arbitrary"),
        name="da_attention",
    )(lam, qk, qk, v, bias_tiles, out_gain)


def _mla_attn_kernel(q_ref, k_ref, v_ref, mask_ref, o_ref, m_ref, l_ref, acc_ref, *, blk, heads):
    qi = pl.program_id(2)
    m_ref[...] = jnp.full(m_ref.shape, MASK_VALUE, F32)
    l_ref[...] = jnp.zeros(l_ref.shape, F32)
    acc_ref[...] = jnp.zeros(acc_ref.shape, F32)

    def step(j, masked):
        start = pl.multiple_of(j * blk, blk)
        for h in range(heads):
            k = k_ref[0, h, pl.ds(start, blk), :]
            v = v_ref[0, h, pl.ds(start, blk), :]
            s = lax.dot_general(q_ref[0, h], k, (((1,), (1,)), ((), ())), preferred_element_type=F32)
            if masked:
                s = s + mask_ref[...]
            _softmax_step(s, v, m_ref, l_ref, acc_ref, h)

    def far_body(j, carry):
        step(j, False)
        return carry

    lax.fori_loop(0, qi, far_body, 0)
    step(qi, True)
    for h in range(heads):
        l_row = jnp.sum(l_ref[h], axis=-1, keepdims=True)
        o_ref[0, :, h * MLA_V:(h + 1) * MLA_V] = (acc_ref[h] * (1.0 / l_row)).astype(o_ref.dtype)


def mla_attention(q, k, v, mask_tile, blk, heads=2):
    b, h, s, _ = q.shape
    kern = functools.partial(_mla_attn_kernel, blk=blk, heads=heads)
    return pl.pallas_call(
        kern,
        grid=(b, h // heads, s // blk),
        in_specs=[pl.BlockSpec((1, heads, blk, MLA_QK_PAD), lambda bi, hi, i: (bi, hi, i, 0)),
                  pl.BlockSpec((1, heads, s, MLA_QK_PAD), lambda bi, hi, i: (bi, hi, 0, 0)),
                  pl.BlockSpec((1, heads, s, MLA_V), lambda bi, hi, i: (bi, hi, 0, 0)),
                  pl.BlockSpec((blk, blk), lambda bi, hi, i: (0, 0))],
        out_specs=pl.BlockSpec((1, blk, heads * MLA_V), lambda bi, hi, i: (bi, i, hi)),
        out_shape=jax.ShapeDtypeStruct((b, s, h * MLA_V), BF16),
        scratch_shapes=[pltpu.VMEM((heads, blk, LANES), F32), pltpu.VMEM((heads, blk, LANES), F32),
                        pltpu.VMEM((heads, blk, MLA_V), F32)],
        compiler_params=_cparams("parallel", "parallel", "arbitrary"),
        name="mla_attention",
    )(q, k, v, mask_tile)


def _bf16_twice(x):
    hi = lax.bitcast_convert_type(x.astype(BF16).astype(F32), jnp.uint32)
    return hi | (hi >> 16)


def _first_max(cur, row_ids, n_rows):
    m = jnp.max(cur, axis=0, keepdims=True)
    first = jnp.min(jnp.where(cur == m, row_ids, n_rows), axis=0, keepdims=True)
    return m, row_ids == first


def _top16(s, want_rank):
    n_rows, tm = s.shape
    row_ids = lax.broadcasted_iota(jnp.int32, (n_rows, tm), 0)
    slot_ids = lax.broadcasted_iota(jnp.int32, (PEER_TOPK, tm), 0)
    cur = s
    rank = jnp.full(s.shape, float(PEER_TOPK), F32)
    vals = jnp.zeros((PEER_TOPK, tm), F32)
    for a in range(PEER_TOPK):
        m, sel = _first_max(cur, row_ids, n_rows)
        if want_rank:
            rank = jnp.where(sel, float(a), rank)
        cur = jnp.where(sel, -jnp.inf, cur)
        vals = jnp.where(slot_ids == a, m, vals)
    return vals, (rank if want_rank else cur)


def _peer_select_kernel(q_ref, keys_ref, n_ref, e1_ref, k_ref, e2_ref):
    tm = q_ref.shape[0]
    q = q_ref[...]

    def scores(c):
        return lax.dot_general(keys_ref[0, c], q[:, c * PEER_KEYS:(c + 1) * PEER_KEYS],
                               (((1,), (1,)), ((), ())), precision=lax.Precision.HIGHEST,
                               preferred_element_type=F32)

    s1, s2 = scores(0), scores(1)
    v1, rank1 = _top16(s1, True)
    v2, rank2 = _top16(s2, True)

    row8 = lax.broadcasted_iota(jnp.int32, (8, tm), 0)
    groups = [v1[0:1] + v2, v1[1:2] + v2[0:8]]
    for a in range(2, 8):
        groups.append(jnp.where(row8 < PEER_TOPK // (a + 1), v1[a:a + 1] + v2[0:8], -jnp.inf))
    groups.append(v1[8:16] + v2[0:1])
    cand = jnp.concatenate(groups, axis=0)
    n_cand = cand.shape[0]
    cand_ids = lax.broadcasted_iota(jnp.int32, (n_cand, tm), 0)
    cmax = v1[0:1] + v2[0:1]
    z = jnp.zeros((1, tm), F32)
    picked = jnp.zeros((n_cand, tm), F32)
    cur = cand
    for _ in range(PEER_TOPK):
        m, sel = _first_max(cur, cand_ids, n_cand)
        z = z + jnp.exp(m - cmax)
        picked = jnp.where(sel, 1.0, picked)
        cur = jnp.where(sel, -jnp.inf, cur)

    n_rows = [jnp.sum(picked[0:16], axis=0, keepdims=True), jnp.sum(picked[16:24], axis=0, keepdims=True)]
    for a in range(2, 8):
        lo = 24 + (a - 2) * 8
        n_rows.append(jnp.sum(picked[lo:lo + 8], axis=0, keepdims=True))
    n_sel = jnp.concatenate(n_rows + [picked[72:80]], axis=0)

    n_of_i = jnp.zeros(s1.shape, F32)
    for a in range(PEER_TOPK):
        n_of_i = jnp.where(rank1 == float(a), n_sel[a:a + 1], n_of_i)

    n_words = _bf16_twice(n_of_i)
    e1_words = _bf16_twice(jnp.exp(s1 - v1[0:1]))
    for lt in range(tm // LANES):
        lanes = slice(lt * LANES, (lt + 1) * LANES)
        n_ref[0, lt] = n_words[:, lanes]
        e1_ref[0, lt] = e1_words[:, lanes]
    k_ref[0] = pltpu.bitcast(rank2.astype(BF16), jnp.uint32)
    e2_ref[0] = pltpu.bitcast((jnp.exp(s2 - v2[0:1]) / z).astype(BF16), jnp.uint32)


def peer_select(q, sub_keys, tm=512):
    t = q.shape[0]
    tm = _tile(t, tm)
    out_i = jax.ShapeDtypeStruct((PEER_HEADS, t // LANES, PEER_KEYS, LANES), jnp.uint32)
    out_j = jax.ShapeDtypeStruct((PEER_HEADS, PEER_KEYS // 2, t), jnp.uint32)
    ispec = pl.BlockSpec((1, tm // LANES, PEER_KEYS, LANES), lambda i, h: (h, i, 0, 0))
    jspec = pl.BlockSpec((1, PEER_KEYS // 2, tm), lambda i, h: (h, 0, i))
    return pl.pallas_call(
        _peer_select_kernel,
        grid=(t // tm, PEER_HEADS),
        in_specs=[pl.BlockSpec((tm, 2 * PEER_KEYS), lambda i, h: (i, h)),
                  pl.BlockSpec((1, 2, PEER_KEYS, PEER_KEYS), lambda i, h: (h, 0, 0, 0))],
        out_specs=[ispec, ispec, jspec, jspec],
        out_shape=[out_i, out_i, out_j, out_j],
        compiler_params=_cparams("parallel", "arbitrary"),
        name="peer_select",
    )(q, sub_keys)


def _peer_expert_kernel(ht_ref, x1_ref, u_ref, vt_ref, n_ref, e1_ref, k_ref, e2_ref, o_ref, acc_ref,
                        *, n_chain, tm):
    eb = pl.program_id(1)

    @pl.when(eb == 0)
    def _():
        acc_ref[...] = jnp.zeros(acc_ref.shape, F32)

    halves = EXPERT_CHAIN // PEER_KEYS
    word_rows = GATE_ROWS // 2
    word_tile = (word_rows, LANES)
    total = None
    for c in range(n_chain):
        rows = slice(c * EXPERT_CHAIN, (c + 1) * EXPERT_CHAIN)
        a = jnp.dot(u_ref[rows, :], ht_ref[...], preferred_element_type=F32)
        w_rows = [[None] * (tm // LANES) for _ in range(EXPERT_CHAIN // GATE_ROWS)]
        for half in range(halves):
            row_i = pl.ds((eb * n_chain + c) * halves + half, 1)
            for lt in range(tm // LANES):
                lanes = slice(lt * LANES, (lt + 1) * LANES)
                n_rows = [pltpu.bitcast(jnp.broadcast_to(n_ref[h, lt, row_i, :], word_tile), BF16)
                          for h in range(PEER_HEADS)]
                e1_rows = [pltpu.bitcast(jnp.broadcast_to(e1_ref[h, lt, row_i, :], word_tile), BF16)
                           for h in range(PEER_HEADS)]
                for jc in range(PEER_KEYS // GATE_ROWS):
                    words = slice(jc * word_rows, (jc + 1) * word_rows)
                    r0 = half * PEER_KEYS + jc * GATE_ROWS
                    a_rows = a[r0:r0 + GATE_ROWS, lanes]
                    act = 0.5 * a_rows * (1.0 + lax.erf(a_rows * INV_SQRT2))
                    gate = jnp.zeros((GATE_ROWS, LANES), BF16)
                    for h in range(PEER_HEADS):
                        k_tile = pltpu.bitcast(k_ref[h, words, lanes], BF16)
                        e2_tile = pltpu.bitcast(e2_ref[h, words, lanes], BF16)
                        gate = gate + jnp.where(k_tile < n_rows[h], e2_tile * e1_rows[h], jnp.zeros((), BF16))
                    w_rows[r0 // GATE_ROWS][lt] = gate * act.astype(BF16)
        w = jnp.concatenate([jnp.concatenate(r, axis=1) for r in w_rows], axis=0)
        part = jnp.dot(vt_ref[:, rows], w, preferred_element_type=F32)
        total = part if total is None else total + part
    acc_ref[...] += total

    @pl.when(eb == pl.num_programs(1) - 1)
    def _():
        o_ref[...] = x1_ref[...] + jnp.transpose(acc_ref[...])


def peer_experts(ht, x1, u, vt, n_i, e1, k_j, e2, tm=512, te=1024):
    d, t = ht.shape
    e = u.shape[0]
    tm, te = _tile(t, tm), _tile(e, te)
    ispec = pl.BlockSpec((PEER_HEADS, tm // LANES, PEER_KEYS, LANES), lambda i, j: (0, i, 0, 0))
    jspec = pl.BlockSpec((PEER_HEADS, PEER_KEYS // 2, tm), lambda i, j: (0, 0, i))
    kern = functools.partial(_peer_expert_kernel, n_chain=te // EXPERT_CHAIN, tm=tm)
    return pl.pallas_call(
        kern,
        grid=(t // tm, e // te),
        in_specs=[pl.BlockSpec((d, tm), lambda i, j: (0, i)),
                  pl.BlockSpec((tm, d), lambda i, j: (i, 0)),
                  pl.BlockSpec((te, d), lambda i, j: (j, 0)),
                  pl.BlockSpec((d, te), lambda i, j: (0, j)),
                  ispec, ispec, jspec, jspec],
        out_specs=pl.BlockSpec((tm, d), lambda i, j: (i, 0)),
        out_shape=jax.ShapeDtypeStruct((t, d), F32),
        scratch_shapes=[pltpu.VMEM((d, tm), F32)],
        compiler_params=_cparams("parallel", "arbitrary"),
        name="peer_experts",
    )(ht, x1, u, vt, n_i, e1, k_j, e2)


def _swap_halves(w):
    half = w.shape[-1] // 2
    return jnp.concatenate([w[..., half:], w[..., :half]], axis=-1)


def _pad_lanes(a):
    pad = (-a.shape[-1]) % LANES
    return jnp.pad(a, [(0, 0)] * (a.ndim - 1) + [(0, pad)])


def _t5_bucket(dist):
    n = jnp.maximum(dist, 0)
    max_exact = REL_BUCKETS // 2
    nf = jnp.maximum(n, max_exact).astype(F32)
    large = max_exact + (jnp.log(nf / max_exact) / math.log(REL_MAX_DISTANCE / max_exact)
                         * (REL_BUCKETS - max_exact)).astype(jnp.int32)
    large = jnp.minimum(large, REL_BUCKETS - 1)
    return jnp.where(n < max_exact, n, large)


def _bias_tiles(rel_bias, blk):
    assert blk >= REL_MAX_DISTANCE
    n = blk
    dist = jnp.arange(2 * n, dtype=jnp.int32)
    by_dist = jnp.take(rel_bias, _t5_bucket(dist), axis=0) - rel_bias[REL_BUCKETS - 1]
    by_dist = jnp.transpose(by_dist * LOG2E).astype(F32)
    h = by_dist.shape[0]
    desc = by_dist[:, n:0:-1]
    masked = jnp.full((h, n), MASK_VALUE, F32)
    far = jnp.zeros((h, 1), F32)
    w_diag = jnp.concatenate([by_dist[:, :1], masked, desc], axis=1)
    w_below = jnp.concatenate([desc, far, far, by_dist[:, :n:-1]], axis=1)

    def toeplitz(w):
        return jnp.tile(w, (1, n))[:, :n * 2 * n].reshape(h, n, 2 * n)[:, :, :n]

    return jnp.stack([toeplitz(w_diag), toeplitz(w_below)], axis=1)


def kernel(x, w_in, attn_norm, da_q_norm, da_k_norm, da_lambda_q1, da_lambda_k1, da_lambda_q2, da_lambda_k2, da_out_norm, da_proj, mla_q_latent_norm, mla_w_uq, mla_kv_latent_norm, mla_w_ukv, mla_q_norm, mla_k_norm, mla_proj, w_out, rel_bias, ffn_norm, peer_w_query, peer_sub_keys, peer_u, peer_v):
    b, s, d = x.shape
    t = b * s
    blk = _tile(s, 512)
    x2 = x.reshape(t, d)

    l = 0

    pos = jnp.arange(s, dtype=F32)
    inv_freq = ROPE_THETA ** (-jnp.arange(0, MLA_ROPE, 2, dtype=F32) / MLA_ROPE)
    ang = pos[:, None] * inv_freq[None, :]
    cos, sin = jnp.cos(ang), jnp.sin(ang)
    cos_t = _pad_lanes(jnp.concatenate([cos, cos], axis=1))
    sin_t = _pad_lanes(jnp.concatenate([-sin, sin], axis=1))

    w = w_in[l]
    c_k = 2 * DA_HEADS * 2 * DA_HEAD_DIM
    c_v = c_k + DA_HEADS * DA_V_DIM
    c_ckv = c_v + MLA_Q_RANK + MLA_KV_RANK
    c_kr = c_ckv + MLA_ROPE
    w_kr = w[:, c_ckv:c_kr]
    w_lat = jnp.concatenate([w[:, c_v:c_ckv], _pad_lanes(w_kr), _pad_lanes(_swap_halves(w_kr))], axis=1)
    n_halves = DA_HEADS * 2
    da_gain = jnp.concatenate([jnp.tile(da_q_norm[l], n_halves) * (DA_HEAD_DIM ** -0.5 * LOG2E),
                               jnp.tile(da_k_norm[l], n_halves)])
    wq = mla_w_uq[l].reshape(MLA_Q_RANK, MLA_HEADS, MLA_QK)
    wq_rope = wq[..., MLA_NOPE:]
    wq = jnp.concatenate([wq[..., :MLA_NOPE], _pad_lanes(wq_rope), _pad_lanes(_swap_halves(wq_rope))], axis=-1)
    wq = jnp.transpose(wq, (1, 0, 2)).astype(BF16)
    wkv = jnp.transpose(mla_w_ukv[l].reshape(MLA_KV_RANK, MLA_HEADS, MLA_NOPE + MLA_V), (1, 0, 2)).astype(BF16)
    gq = (_pad_lanes(mla_q_norm[l]) * (MLA_QK ** -0.5 * LOG2E)).reshape(1, MLA_QK_PAD).astype(F32)
    gk = _pad_lanes(mla_k_norm[l]).reshape(1, MLA_QK_PAD).astype(F32)

    qk = in_proj(x2, attn_norm[l], w[:, :c_k].astype(BF16), col_gain=da_gain).reshape(b, s, c_k)
    va = in_proj(x2, attn_norm[l], w[:, c_k:c_v].astype(BF16)).reshape(b, s, c_v - c_k)
    gates = in_proj(x2, attn_norm[l], w[:, c_kr:].astype(BF16), sigmoid=True)
    cqn, ckvn, kr = latent_proj(x2, attn_norm[l], w_lat.astype(BF16), mla_q_latent_norm[l],
                                mla_kv_latent_norm[l], cos_t, sin_t, s)

    lam = (jnp.exp(jnp.sum(da_lambda_q1[l] * da_lambda_k1[l]))
           - jnp.exp(jnp.sum(da_lambda_q2[l] * da_lambda_k2[l])) + LAMBDA_INIT).reshape(1).astype(F32)
    out_gain = (da_out_norm[l] * (1.0 - LAMBDA_INIT)).reshape(1, DA_V_DIM).astype(F32)
    ya = da_attention(qk, va, _bias_tiles(rel_bias, blk), lam, out_gain, blk)

    qb = mla_q_proj(cqn, wq, cos_t, sin_t, gq, b, s)
    kb, vb = mla_kv_proj(ckvn, wkv, kr, gk, b, s)
    causal = np.arange(blk)[:, None] >= np.arange(blk)[None, :]
    mask_tile = jnp.asarray(np.where(causal, 0.0, MASK_VALUE).astype(np.float32))
    yb = mla_attention(qb, kb, vb, mask_tile, blk)

    x1, ht, qp = merge_and_query(ya.reshape(t, -1), yb.reshape(t, -1), gates, x2, da_proj[l].astype(BF16),
                                 mla_proj[l].astype(BF16), w_out[l].astype(BF16), ffn_norm[l],
                                 peer_w_query[l].astype(BF16))

    n_i, e1, k_j, e2 = peer_select(qp, peer_sub_keys[l])
    u = peer_u[l].astype(BF16)
    vt = jnp.transpose(peer_v[l]).astype(BF16)
    return peer_experts(ht, x1, u, vt, n_i, e1, k_j, e2).reshape(b, s, d)
```

```python
import functools
import math

import numpy as np
import jax
import jax.numpy as jnp
from jax import lax
from jax.experimental import pallas as pl
from jax.experimental.pallas import tpu as pltpu

F32 = jnp.float32
BF16 = jnp.bfloat16

EPS = 1e-6
DA_HEADS = 4
DA_HEAD_DIM = 128
DA_V_DIM = 256
MLA_HEADS = 8
MLA_NOPE = 128
MLA_ROPE = 64
MLA_QK = MLA_NOPE + MLA_ROPE
MLA_QK_PAD = 256
MLA_V = 128
MLA_Q_RANK = 384
MLA_KV_RANK = 256
ROPE_THETA = 10000.0
REL_BUCKETS = 32
REL_MAX_DISTANCE = 128
PEER_HEADS = 8
PEER_KEYS = 128
PEER_TOPK = 16
LAMBDA_INIT = 0.8 - 0.6 * math.exp(-0.3 * 0)
MASK_VALUE = -1e30
INV_SQRT2 = 0.7071067811865476
LOG2E = 1.4426950408889634
LANES = 128
SUBLANES = 8
EXPERT_CHAIN = 256
GATE_ROWS = 16

VMEM_LIMIT = 56 * 1024 * 1024


def _cparams(*sem):
    return pltpu.CompilerParams(dimension_semantics=sem, vmem_limit_bytes=VMEM_LIMIT)


def _tile(n, pref):
    t = min(n, pref)
    assert n % t == 0, (n, t)
    return t


def _rms_rows(x, gain_row, width=None):
    ms = jnp.sum(x * x, axis=-1, keepdims=True) * (1.0 / (width or x.shape[-1]))
    return x * lax.rsqrt(ms + EPS) * gain_row


def _in_proj_kernel(x_ref, g_ref, w_ref, o_ref, xn_ref, *, sigmoid):
    @pl.when(pl.program_id(1) == 0)
    def _():
        xn_ref[...] = _rms_rows(x_ref[...], g_ref[...]).astype(BF16)

    y = jnp.dot(xn_ref[...], w_ref[...], preferred_element_type=F32)
    if sigmoid:
        y = 1.0 / (1.0 + jnp.exp(-y))
    o_ref[...] = y.astype(o_ref.dtype)


def _in_proj_head_norm_kernel(x_ref, g_ref, w_ref, cg_ref, o_ref, xn_ref):
    @pl.when(pl.program_id(1) == 0)
    def _():
        xn_ref[...] = _rms_rows(x_ref[...], g_ref[...]).astype(BF16)

    y = jnp.dot(xn_ref[...], w_ref[...], preferred_element_type=F32)
    for c in range(y.shape[1] // LANES):
        cols = slice(c * LANES, (c + 1) * LANES)
        o_ref[:, cols] = _rms_rows(y[:, cols], cg_ref[:, cols]).astype(o_ref.dtype)


def in_proj(x, g, w, col_gain=None, sigmoid=False, tm=1024, tn=512):
    t, k = x.shape
    n = w.shape[1]
    tm, tn = _tile(t, tm), _tile(n, tn)
    in_specs = [pl.BlockSpec((tm, k), lambda i, j: (i, 0)),
                pl.BlockSpec((1, k), lambda i, j: (0, 0)),
                pl.BlockSpec((k, tn), lambda i, j: (0, j))]
    args = [x, g.reshape(1, k).astype(F32), w]
    if col_gain is None:
        kern = functools.partial(_in_proj_kernel, sigmoid=sigmoid)
    else:
        assert not sigmoid
        kern = _in_proj_head_norm_kernel
        in_specs.append(pl.BlockSpec((1, tn), lambda i, j: (0, j)))
        args.append(col_gain.reshape(1, n).astype(F32))
    return pl.pallas_call(
        kern,
        grid=(t // tm, n // tn),
        in_specs=in_specs,
        out_specs=pl.BlockSpec((tm, tn), lambda i, j: (i, j)),
        out_shape=jax.ShapeDtypeStruct((t, n), BF16),
        scratch_shapes=[pltpu.VMEM((tm, k), BF16)],
        compiler_params=_cparams("parallel", "arbitrary"),
        name="in_proj",
    )(*args)


def _latent_kernel(x_ref, g_ref, w_ref, gq_ref, gkv_ref, cos_ref, sin_ref, cq_ref, ckv_ref, kr_ref):
    xn = _rms_rows(x_ref[...], g_ref[...]).astype(BF16)
    y = jnp.dot(xn, w_ref[...], preferred_element_type=F32)
    c0, c1 = MLA_Q_RANK, MLA_Q_RANK + MLA_KV_RANK
    cq_ref[...] = _rms_rows(y[:, :c0], gq_ref[...]).astype(BF16)
    ckv_ref[...] = _rms_rows(y[:, c0:c1], gkv_ref[...]).astype(BF16)
    kr_ref[...] = y[:, c1:c1 + LANES] * cos_ref[...] + y[:, c1 + LANES:] * sin_ref[...]


def latent_proj(x, g, w, gq, gkv, cos_t, sin_t, seq, tm=1024):
    t, k = x.shape
    tm = _tile(seq, tm)
    spb = seq // tm
    n = w.shape[1]
    row = lambda a: a.reshape(1, -1).astype(F32)
    return pl.pallas_call(
        _latent_kernel,
        grid=(t // tm,),
        in_specs=[pl.BlockSpec((tm, k), lambda i: (i, 0)),
                  pl.BlockSpec((1, k), lambda i: (0, 0)),
                  pl.BlockSpec((k, n), lambda i: (0, 0)),
                  pl.BlockSpec((1, MLA_Q_RANK), lambda i: (0, 0)),
                  pl.BlockSpec((1, MLA_KV_RANK), lambda i: (0, 0)),
                  pl.BlockSpec((tm, LANES), lambda i: (i % spb, 0)),
                  pl.BlockSpec((tm, LANES), lambda i: (i % spb, 0))],
        out_specs=[pl.BlockSpec((tm, MLA_Q_RANK), lambda i: (i, 0)),
                   pl.BlockSpec((tm, MLA_KV_RANK), lambda i: (i, 0)),
                   pl.BlockSpec((tm, LANES), lambda i: (i, 0))],
        out_shape=[jax.ShapeDtypeStruct((t, MLA_Q_RANK), BF16),
                   jax.ShapeDtypeStruct((t, MLA_KV_RANK), BF16),
                   jax.ShapeDtypeStruct((t, LANES), F32)],
        compiler_params=_cparams("parallel"),
        name="latent_proj",
    )(x, row(g), w, row(gq), row(gkv), cos_t, sin_t)


def _mla_q_kernel(cq_ref, w_ref, cos_ref, sin_ref, g_ref, o_ref):
    y = jnp.dot(cq_ref[...], w_ref[0], preferred_element_type=F32)
    nope = y[:, :LANES]
    rope = y[:, LANES:2 * LANES] * cos_ref[...] + y[:, 2 * LANES:] * sin_ref[...]
    ms = (jnp.sum(nope * nope, axis=-1, keepdims=True)
          + jnp.sum(rope * rope, axis=-1, keepdims=True)) * (1.0 / MLA_QK)
    sc = lax.rsqrt(ms + EPS)
    o_ref[0, 0, :, :LANES] = (nope * sc * g_ref[:, :LANES]).astype(BF16)
    o_ref[0, 0, :, LANES:] = (rope * sc * g_ref[:, LANES:]).astype(BF16)


def mla_q_proj(cq, w, cos_t, sin_t, gain, batch, seq, tm=1024):
    tm = _tile(seq, tm)
    spb = seq // tm
    h, r, n = w.shape
    return pl.pallas_call(
        _mla_q_kernel,
        grid=(batch * spb, h),
        in_specs=[pl.BlockSpec((tm, r), lambda i, hi: (i, 0)),
                  pl.BlockSpec((1, r, n), lambda i, hi: (hi, 0, 0)),
                  pl.BlockSpec((tm, LANES), lambda i, hi: (i % spb, 0)),
                  pl.BlockSpec((tm, LANES), lambda i, hi: (i % spb, 0)),
                  pl.BlockSpec((1, MLA_QK_PAD), lambda i, hi: (0, 0))],
        out_specs=pl.BlockSpec((1, 1, tm, MLA_QK_PAD), lambda i, hi: (i // spb, hi, i % spb, 0)),
        out_shape=jax.ShapeDtypeStruct((batch, h, seq, MLA_QK_PAD), BF16),
        compiler_params=_cparams("parallel", "arbitrary"),
        name="mla_q_proj",
    )(cq, w, cos_t, sin_t, gain)


def _mla_kv_kernel(ckv_ref, w_ref, kr_ref, g_ref, k_ref, v_ref):
    y = jnp.dot(ckv_ref[...], w_ref[0], preferred_element_type=F32)
    nope = y[:, :LANES]
    kr = kr_ref[...]
    ms = (jnp.sum(nope * nope, axis=-1, keepdims=True)
          + jnp.sum(kr * kr, axis=-1, keepdims=True)) * (1.0 / MLA_QK)
    sc = lax.rsqrt(ms + EPS)
    k_ref[0, 0, :, :LANES] = (nope * sc * g_ref[:, :LANES]).astype(BF16)
    k_ref[0, 0, :, LANES:] = (kr * sc * g_ref[:, LANES:]).astype(BF16)
    v_ref[0, 0] = y[:, LANES:].astype(BF16)


def mla_kv_proj(ckv, w, kr, gain, batch, seq, tm=1024):
    tm = _tile(seq, tm)
    spb = seq // tm
    h, r, n = w.shape
    return pl.pallas_call(
        _mla_kv_kernel,
        grid=(batch * spb, h),
        in_specs=[pl.BlockSpec((tm, r), lambda i, hi: (i, 0)),
                  pl.BlockSpec((1, r, n), lambda i, hi: (hi, 0, 0)),
                  pl.BlockSpec((tm, LANES), lambda i, hi: (i, 0)),
                  pl.BlockSpec((1, MLA_QK_PAD), lambda i, hi: (0, 0))],
        out_specs=[pl.BlockSpec((1, 1, tm, MLA_QK_PAD), lambda i, hi: (i // spb, hi, i % spb, 0)),
                   pl.BlockSpec((1, 1, tm, MLA_V), lambda i, hi: (i // spb, hi, i % spb, 0))],
        out_shape=[jax.ShapeDtypeStruct((batch, h, seq, MLA_QK_PAD), BF16),
                   jax.ShapeDtypeStruct((batch, h, seq, MLA_V), BF16)],
        compiler_params=_cparams("parallel", "arbitrary"),
        name="mla_kv_proj",
    )(ckv, w, kr, gain)


def _merge_kernel(ya_ref, yb_ref, gate_ref, x_ref, wa_ref, wb_ref, wo_ref, fg_ref, wq_ref,
                  x1_ref, ht_ref, qp_ref):
    d = x_ref.shape[1]
    ya = jnp.dot(ya_ref[...], wa_ref[...], preferred_element_type=F32)
    yb = jnp.dot(yb_ref[...], wb_ref[...], preferred_element_type=F32)
    merged = gate_ref[:, :d].astype(F32) * ya + gate_ref[:, d:].astype(F32) * yb
    x1 = x_ref[...] + jnp.dot(merged.astype(BF16), wo_ref[...], preferred_element_type=F32)
    x1_ref[...] = x1
    h2 = _rms_rows(x1, fg_ref[...])
    ht_ref[...] = jnp.transpose(h2).astype(BF16)
    qp_ref[...] = jnp.dot(h2.astype(BF16), wq_ref[...], preferred_element_type=F32)


def merge_and_query(ya, yb, gates, x, w_da, w_mla, w_out, ffn_gain, w_query, tm=256):
    t, d = x.shape
    tm = _tile(t, tm)
    nq = w_query.shape[1]
    full = lambda a: pl.BlockSpec(a.shape, lambda i: (0, 0))
    rows = lambda n: pl.BlockSpec((tm, n), lambda i: (i, 0))
    fg = ffn_gain.reshape(1, d).astype(F32)
    return pl.pallas_call(
        _merge_kernel,
        grid=(t // tm,),
        in_specs=[rows(ya.shape[1]), rows(yb.shape[1]), rows(gates.shape[1]), rows(d),
                  full(w_da), full(w_mla), full(w_out), full(fg), full(w_query)],
        out_specs=[rows(d), pl.BlockSpec((d, tm), lambda i: (0, i)), rows(nq)],
        out_shape=[jax.ShapeDtypeStruct((t, d), F32), jax.ShapeDtypeStruct((d, t), BF16),
                   jax.ShapeDtypeStruct((t, nq), F32)],
        compiler_params=_cparams("parallel"),
        name="merge_and_query",
    )(ya, yb, gates, x, w_da, w_mla, w_out, fg, w_query)


def _lane_tile(x, reps):
    return x if reps == 1 else jnp.concatenate([x] * reps, axis=1)


def _softmax_step(s, v, m_ref, l_ref, acc_ref, idx):
    tk, dv = s.shape[1], v.shape[1]
    m_old = m_ref[idx]
    m_new = jnp.maximum(m_old, jnp.max(s, axis=-1, keepdims=True))
    p = jnp.exp2(s - _lane_tile(m_new, tk // LANES))
    alpha = jnp.exp2(m_old - m_new)
    p_lanes = p[:, :LANES]
    for c in range(1, tk // LANES):
        p_lanes = p_lanes + p[:, c * LANES:(c + 1) * LANES]
    l_ref[idx] = alpha * l_ref[idx] + p_lanes
    acc_ref[idx] = (_lane_tile(alpha, dv // LANES) * acc_ref[idx]
                    + jnp.dot(p.astype(BF16), v, preferred_element_type=F32))
    m_ref[idx] = m_new


def _da_attn_kernel(lam_ref, q_ref, k_ref, v_ref, bias_ref, g_ref, o_ref, m_ref, l_ref, acc_ref, *, blk):
    qi = pl.program_id(2)
    m_ref[...] = jnp.full(m_ref.shape, MASK_VALUE, F32)
    l_ref[...] = jnp.zeros(l_ref.shape, F32)
    acc_ref[...] = jnp.zeros(acc_ref.shape, F32)

    def step(j, bias_idx):
        start = pl.multiple_of(j * blk, blk)
        kblk = k_ref[0, pl.ds(start, blk), :]
        vblk = v_ref[0, pl.ds(start, blk), :]
        for mp in range(2):
            q = q_ref[0, :, mp * DA_HEAD_DIM:(mp + 1) * DA_HEAD_DIM]
            k = kblk[:, mp * DA_HEAD_DIM:(mp + 1) * DA_HEAD_DIM]
            s = lax.dot_general(q, k, (((1,), (1,)), ((), ())), preferred_element_type=F32)
            if bias_idx is not None:
                s = s + bias_ref[0, bias_idx]
            _softmax_step(s, vblk, m_ref, l_ref, acc_ref, mp)

    def far_body(j, carry):
        step(j, None)
        return carry

    lax.fori_loop(0, jnp.maximum(qi - 1, 0), far_body, 0)

    @pl.when(qi >= 1)
    def _():
        step(qi - 1, 1)

    step(qi, 0)

    o = (acc_ref[0] * (1.0 / jnp.sum(l_ref[0], axis=-1, keepdims=True))
         - acc_ref[1] * (lam_ref[0] / jnp.sum(l_ref[1], axis=-1, keepdims=True)))
    ms = jnp.mean(o * o, axis=-1, keepdims=True)
    o_ref[0] = (o * lax.rsqrt(ms + EPS) * g_ref[...]).astype(o_ref.dtype)


def da_attention(qk, v, bias_tiles, lam, out_gain, blk):
    b, s, _ = qk.shape
    hw = 2 * DA_HEAD_DIM
    kern = functools.partial(_da_attn_kernel, blk=blk)
    return pl.pallas_call(
        kern,
        grid=(b, DA_HEADS, s // blk),
        in_specs=[pl.BlockSpec(memory_space=pltpu.SMEM),
                  pl.BlockSpec((1, blk, hw), lambda bi, h, i: (bi, i, h)),
                  pl.BlockSpec((1, s, hw), lambda bi, h, i: (bi, 0, DA_HEADS + h)),
                  pl.BlockSpec((1, s, DA_V_DIM), lambda bi, h, i: (bi, 0, h)),
                  pl.BlockSpec((1, 2, blk, blk), lambda bi, h, i: (h, 0, 0, 0)),
                  pl.BlockSpec((1, DA_V_DIM), lambda bi, h, i: (0, 0))],
        out_specs=pl.BlockSpec((1, blk, DA_V_DIM), lambda bi, h, i: (bi, i, h)),
        out_shape=jax.ShapeDtypeStruct((b, s, DA_HEADS * DA_V_DIM), BF16),
        scratch_shapes=[pltpu.VMEM((2, blk, LANES), F32), pltpu.VMEM((2, blk, LANES), F32),
                        pltpu.VMEM((2, blk, DA_V_DIM), F32)],
        compiler_params=_cparams("parallel", "parallel", "arbitrary"),
        name="da_attention",
    )(lam, qk, qk, v, bias_tiles, out_gain)


def _mla_attn_kernel(q_ref, k_ref, v_ref, mask_ref, o_ref, m_ref, l_ref, acc_ref, *, blk, heads):
    qi = pl.program_id(2)
    m_ref[...] = jnp.full(m_ref.shape, MASK_VALUE, F32)
    l_ref[...] = jnp.zeros(l_ref.shape, F32)
    acc_ref[...] = jnp.zeros(acc_ref.shape, F32)

    def step(j, masked):
        start = pl.multiple_of(j * blk, blk)
        for h in range(heads):
            k = k_ref[0, h, pl.ds(start, blk), :]
            v = v_ref[0, h, pl.ds(start, blk), :]
            s = lax.dot_general(q_ref[0, h], k, (((1,), (1,)), ((), ())), preferred_element_type=F32)
            if masked:
                s = s + mask_ref[...]
            _softmax_step(s, v, m_ref, l_ref, acc_ref, h)

    def far_body(j, carry):
        step(j, False)
        return carry

    lax.fori_loop(0, qi, far_body, 0)
    step(qi, True)
    for h in range(heads):
        l_row = jnp.sum(l_ref[h], axis=-1, keepdims=True)
        o_ref[0, :, h * MLA_V:(h + 1) * MLA_V] = (acc_ref[h] * (1.0 / l_row)).astype(o_ref.dtype)


def mla_attention(q, k, v, mask_tile, blk, heads=2):
    b, h, s, _ = q.shape
    kern = functools.partial(_mla_attn_kernel, blk=blk, heads=heads)
    return pl.pallas_call(
        kern,
        grid=(b, h // heads, s // blk),
        in_specs=[pl.BlockSpec((1, heads, blk, MLA_QK_PAD), lambda bi, hi, i: (bi, hi, i, 0)),
                  pl.BlockSpec((1, heads, s, MLA_QK_PAD), lambda bi, hi, i: (bi, hi, 0, 0)),
                  pl.BlockSpec((1, heads, s, MLA_V), lambda bi, hi, i: (bi, hi, 0, 0)),
                  pl.BlockSpec((blk, blk), lambda bi, hi, i: (0, 0))],
        out_specs=pl.BlockSpec((1, blk, heads * MLA_V), lambda bi, hi, i: (bi, i, hi)),
        out_shape=jax.ShapeDtypeStruct((b, s, h * MLA_V), BF16),
        scratch_shapes=[pltpu.VMEM((heads, blk, LANES), F32), pltpu.VMEM((heads, blk, LANES), F32),
                        pltpu.VMEM((heads, blk, MLA_V), F32)],
        compiler_params=_cparams("parallel", "parallel", "arbitrary"),
        name="mla_attention",
    )(q, k, v, mask_tile)


def _bf16_twice(x):
    hi = lax.bitcast_convert_type(x.astype(BF16).astype(F32), jnp.uint32)
    return hi | (hi >> 16)


def _first_max(cur, row_ids, n_rows):
    m = jnp.max(cur, axis=0, keepdims=True)
    first = jnp.min(jnp.where(cur == m, row_ids, n_rows), axis=0, keepdims=True)
    return m, row_ids == first


def _top16_exact(s):
    n_rows, tm = s.shape
    row_ids = lax.broadcasted_iota(jnp.int32, (n_rows, tm), 0)
    slot_ids = lax.broadcasted_iota(jnp.int32, (PEER_TOPK, tm), 0)
    cur = s
    rank = jnp.full(s.shape, float(PEER_TOPK), F32)
    vals = jnp.zeros((PEER_TOPK, tm), F32)
    for a in range(PEER_TOPK):
        m, sel = _first_max(cur, row_ids, n_rows)
        rank = jnp.where(sel, float(a), rank)
        cur = jnp.where(sel, -jnp.inf, cur)
        vals = jnp.where(slot_ids == a, m, vals)
    return vals, rank


def _top16_distinct(s_ref, vals_ref, spread_ref, rank_ref):
    tm = s_ref.shape[2]
    slot_ids = lax.broadcasted_iota(jnp.int32, (PEER_TOPK, tm), 0)
    first = [jnp.max(s_ref[c], axis=0, keepdims=True) for c in range(2)]

    def next_max(a, carry):
        out = []
        for c in range(2):
            m, vals = carry[c]
            s = s_ref[c]
            m = jnp.max(jnp.where(s < m, s, -jnp.inf), axis=0, keepdims=True)
            out.append((m, jnp.where(slot_ids == a, m, vals)))
        return tuple(out)

    done = lax.fori_loop(1, PEER_TOPK, next_max,
                         tuple((m, jnp.where(slot_ids == 0, m, 0.0)) for m in first))
    for c in range(2):
        vals_ref[c] = done[c][1]
        for a in range(PEER_TOPK):
            spread_ref[c, a] = jnp.broadcast_to(done[c][1][a:a + 1], (SUBLANES, tm))

    def rank_rows(g, ranked):
        rows = pl.ds(pl.multiple_of(g * SUBLANES, SUBLANES), SUBLANES)
        out = []
        for c in range(2):
            s = s_ref[c, rows, :]
            rank = jnp.zeros(s.shape, F32)
            for a in range(PEER_TOPK):
                rank = jnp.where(spread_ref[c, a] > s, float(a + 1), rank)
            rank_ref[c, rows, :] = rank
            out.append(ranked[c] + jnp.where(rank < float(PEER_TOPK), 1.0, 0.0))
        return tuple(out)

    zeros = jnp.zeros((SUBLANES, tm), F32)
    ranked = lax.fori_loop(0, s_ref.shape[1] // SUBLANES, rank_rows, (zeros, zeros))
    return jnp.max(jnp.sum(ranked[0], axis=0, keepdims=True) + jnp.sum(ranked[1], axis=0, keepdims=True))


def _peer_select_kernel(q_ref, keys_ref, n_ref, e1_ref, k_ref, e2_ref, s_ref, vals_ref, spread_ref, rank_ref):
    tm = q_ref.shape[0]
    q = q_ref[...]
    for c in range(2):
        s_ref[c] = lax.dot_general(keys_ref[0, c], q[:, c * PEER_KEYS:(c + 1) * PEER_KEYS],
                                   (((1,), (1,)), ((), ())), precision=lax.Precision.HIGHEST,
                                   preferred_element_type=F32)

    most_ranked = _top16_distinct(s_ref, vals_ref, spread_ref, rank_ref)
    s1, s2 = s_ref[0], s_ref[1]
    tables = lax.cond(most_ranked > float(2 * PEER_TOPK),
                      lambda: _gate_tables(s1, s2, *_top16_exact(s1), *_top16_exact(s2)),
                      lambda: _gate_tables(s1, s2, vals_ref[0], rank_ref[0], vals_ref[1], rank_ref[1]))
    n_words, e1_words, k_words, e2_words = tables
    for lt in range(tm // LANES):
        lanes = slice(lt * LANES, (lt + 1) * LANES)
        n_ref[0, lt] = n_words[:, lanes]
        e1_ref[0, lt] = e1_words[:, lanes]
    k_ref[0] = k_words
    e2_ref[0] = e2_words


def _gate_tables(s1, s2, v1, rank1, v2, rank2):
    tm = s1.shape[1]

    row8 = lax.broadcasted_iota(jnp.int32, (8, tm), 0)
    groups = [v1[0:1] + v2, v1[1:2] + v2[0:8]]
    for a in range(2, 8):
        groups.append(jnp.where(row8 < PEER_TOPK // (a + 1), v1[a:a + 1] + v2[0:8], -jnp.inf))
    groups.append(v1[8:16] + v2[0:1])
    cand = jnp.concatenate(groups, axis=0)
    n_cand = cand.shape[0]
    cand_ids = lax.broadcasted_iota(jnp.int32, (n_cand, tm), 0)
    cmax = v1[0:1] + v2[0:1]
    z = jnp.zeros((1, tm), F32)
    picked = jnp.zeros((n_cand, tm), F32)
    cur = cand
    for _ in range(PEER_TOPK):
        m, sel = _first_max(cur, cand_ids, n_cand)
        z = z + jnp.exp(m - cmax)
        picked = jnp.where(sel, 1.0, picked)
        cur = jnp.where(sel, -jnp.inf, cur)

    n_rows = [jnp.sum(picked[0:16], axis=0, keepdims=True), jnp.sum(picked[16:24], axis=0, keepdims=True)]
    for a in range(2, 8):
        lo = 24 + (a - 2) * 8
        n_rows.append(jnp.sum(picked[lo:lo + 8], axis=0, keepdims=True))
    n_sel = jnp.concatenate(n_rows + [picked[72:80]], axis=0)

    n_of_i = jnp.zeros(s1.shape, F32)
    for a in range(PEER_TOPK):
        n_of_i = jnp.where(rank1 == float(a), n_sel[a:a + 1], n_of_i)

    return (_bf16_twice(n_of_i), _bf16_twice(jnp.exp(s1 - v1[0:1])),
            pltpu.bitcast(rank2.astype(BF16), jnp.uint32),
            pltpu.bitcast((jnp.exp(s2 - v2[0:1]) / z).astype(BF16), jnp.uint32))


def peer_select(q, sub_keys, tm=512):
    t = q.shape[0]
    tm = _tile(t, tm)
    out_i = jax.ShapeDtypeStruct((PEER_HEADS, t // LANES, PEER_KEYS, LANES), jnp.uint32)
    out_j = jax.ShapeDtypeStruct((PEER_HEADS, PEER_KEYS // 2, t), jnp.uint32)
    ispec = pl.BlockSpec((1, tm // LANES, PEER_KEYS, LANES), lambda i, h: (h, i, 0, 0))
    jspec = pl.BlockSpec((1, PEER_KEYS // 2, tm), lambda i, h: (h, 0, i))
    return pl.pallas_call(
        _peer_select_kernel,
        grid=(t // tm, PEER_HEADS),
        in_specs=[pl.BlockSpec((tm, 2 * PEER_KEYS), lambda i, h: (i, h)),
                  pl.BlockSpec((1, 2, PEER_KEYS, PEER_KEYS), lambda i, h: (h, 0, 0, 0))],
        out_specs=[ispec, ispec, jspec, jspec],
        out_shape=[out_i, out_i, out_j, out_j],
        scratch_shapes=[pltpu.VMEM((2, PEER_KEYS, tm), F32), pltpu.VMEM((2, PEER_TOPK, tm), F32),
                        pltpu.VMEM((2, PEER_TOPK, SUBLANES, tm), F32), pltpu.VMEM((2, PEER_KEYS, tm), F32)],
        compiler_params=_cparams("parallel", "arbitrary"),
        name="peer_select",
    )(q, sub_keys)


def _peer_expert_kernel(ht_ref, x1_ref, u_ref, vt_ref, n_ref, e1_ref, k_ref, e2_ref, o_ref, acc_ref,
                        *, n_chain, tm):
    eb = pl.program_id(1)

    @pl.when(eb == 0)
    def _():
        acc_ref[...] = jnp.zeros(acc_ref.shape, F32)

    halves = EXPERT_CHAIN // PEER_KEYS
    word_rows = GATE_ROWS // 2
    word_tile = (word_rows, LANES)
    total = None
    for c in range(n_chain):
        rows = slice(c * EXPERT_CHAIN, (c + 1) * EXPERT_CHAIN)
        a = jnp.dot(u_ref[rows, :], ht_ref[...], preferred_element_type=F32)
        w_rows = [[None] * (tm // LANES) for _ in range(EXPERT_CHAIN // GATE_ROWS)]
        for half in range(halves):
            row_i = pl.ds((eb * n_chain + c) * halves + half, 1)
            for lt in range(tm // LANES):
                lanes = slice(lt * LANES, (lt + 1) * LANES)
                n_rows = [pltpu.bitcast(jnp.broadcast_to(n_ref[h, lt, row_i, :], word_tile), BF16)
                          for h in range(PEER_HEADS)]
                e1_rows = [pltpu.bitcast(jnp.broadcast_to(e1_ref[h, lt, row_i, :], word_tile), BF16)
                           for h in range(PEER_HEADS)]
                for jc in range(PEER_KEYS // GATE_ROWS):
                    words = slice(jc * word_rows, (jc + 1) * word_rows)
                    r0 = half * PEER_KEYS + jc * GATE_ROWS
                    a_rows = a[r0:r0 + GATE_ROWS, lanes]
                    act = 0.5 * a_rows * (1.0 + lax.erf(a_rows * INV_SQRT2))
                    gate = jnp.zeros((GATE_ROWS, LANES), BF16)
                    for h in range(PEER_HEADS):
                        k_tile = pltpu.bitcast(k_ref[h, words, lanes], BF16)
                        e2_tile = pltpu.bitcast(e2_ref[h, words, lanes], BF16)
                        gate = gate + jnp.where(k_tile < n_rows[h], e2_tile * e1_rows[h], jnp.zeros((), BF16))
                    w_rows[r0 // GATE_ROWS][lt] = gate * act.astype(BF16)
        w = jnp.concatenate([jnp.concatenate(r, axis=1) for r in w_rows], axis=0)
        part = jnp.dot(vt_ref[:, rows], w, preferred_element_type=F32)
        total = part if total is None else total + part
    acc_ref[...] += total

    @pl.when(eb == pl.num_programs(1) - 1)
    def _():
        o_ref[...] = x1_ref[...] + jnp.transpose(acc_ref[...])


def peer_experts(ht, x1, u, vt, n_i, e1, k_j, e2, tm=512, te=1024):
    d, t = ht.shape
    e = u.shape[0]
    tm, te = _tile(t, tm), _tile(e, te)
    ispec = pl.BlockSpec((PEER_HEADS, tm // LANES, PEER_KEYS, LANES), lambda i, j: (0, i, 0, 0))
    jspec = pl.BlockSpec((PEER_HEADS, PEER_KEYS // 2, tm), lambda i, j: (0, 0, i))
    kern = functools.partial(_peer_expert_kernel, n_chain=te // EXPERT_CHAIN, tm=tm)
    return pl.pallas_call(
        kern,
        grid=(t // tm, e // te),
        in_specs=[pl.BlockSpec((d, tm), lambda i, j: (0, i)),
                  pl.BlockSpec((tm, d), lambda i, j: (i, 0)),
                  pl.BlockSpec((te, d), lambda i, j: (j, 0)),
                  pl.BlockSpec((d, te), lambda i, j: (0, j)),
                  ispec, ispec, jspec, jspec],
        out_specs=pl.BlockSpec((tm, d), lambda i, j: (i, 0)),
        out_shape=jax.ShapeDtypeStruct((t, d), F32),
        scratch_shapes=[pltpu.VMEM((d, tm), F32)],
        compiler_params=_cparams("parallel", "arbitrary"),
        name="peer_experts",
    )(ht, x1, u, vt, n_i, e1, k_j, e2)


def _swap_halves(w):
    half = w.shape[-1] // 2
    return jnp.concatenate([w[..., half:], w[..., :half]], axis=-1)


def _pad_lanes(a):
    pad = (-a.shape[-1]) % LANES
    return jnp.pad(a, [(0, 0)] * (a.ndim - 1) + [(0, pad)])


def _t5_bucket(dist):
    n = jnp.maximum(dist, 0)
    max_exact = REL_BUCKETS // 2
    nf = jnp.maximum(n, max_exact).astype(F32)
    large = max_exact + (jnp.log(nf / max_exact) / math.log(REL_MAX_DISTANCE / max_exact)
                         * (REL_BUCKETS - max_exact)).astype(jnp.int32)
    large = jnp.minimum(large, REL_BUCKETS - 1)
    return jnp.where(n < max_exact, n, large)


def _bias_tiles(rel_bias, blk):
    assert blk >= REL_MAX_DISTANCE
    n = blk
    dist = jnp.arange(2 * n, dtype=jnp.int32)
    by_dist = jnp.take(rel_bias, _t5_bucket(dist), axis=0) - rel_bias[REL_BUCKETS - 1]
    by_dist = jnp.transpose(by_dist * LOG2E).astype(F32)
    h = by_dist.shape[0]
    desc = by_dist[:, n:0:-1]
    masked = jnp.full((h, n), MASK_VALUE, F32)
    far = jnp.zeros((h, 1), F32)
    w_diag = jnp.concatenate([by_dist[:, :1], masked, desc], axis=1)
    w_below = jnp.concatenate([desc, far, far, by_dist[:, :n:-1]], axis=1)

    def toeplitz(w):
        return jnp.tile(w, (1, n))[:, :n * 2 * n].reshape(h, n, 2 * n)[:, :, :n]

    return jnp.stack([toeplitz(w_diag), toeplitz(w_below)], axis=1)


def kernel(x, w_in, attn_norm, da_q_norm, da_k_norm, da_lambda_q1, da_lambda_k1, da_lambda_q2, da_lambda_k2, da_out_norm, da_proj, mla_q_latent_norm, mla_w_uq, mla_kv_latent_norm, mla_w_ukv, mla_q_norm, mla_k_norm, mla_proj, w_out, rel_bias, ffn_norm, peer_w_query, peer_sub_keys, peer_u, peer_v):
    b, s, d = x.shape
    t = b * s
    blk = _tile(s, 512)
    x2 = x.reshape(t, d)

    l = 0

    pos = jnp.arange(s, dtype=F32)
    inv_freq = ROPE_THETA ** (-jnp.arange(0, MLA_ROPE, 2, dtype=F32) / MLA_ROPE)
    ang = pos[:, None] * inv_freq[None, :]
    cos, sin = jnp.cos(ang), jnp.sin(ang)
    cos_t = _pad_lanes(jnp.concatenate([cos, cos], axis=1))
    sin_t = _pad_lanes(jnp.concatenate([-sin, sin], axis=1))

    w = w_in[l]
    c_k = 2 * DA_HEADS * 2 * DA_HEAD_DIM
    c_v = c_k + DA_HEADS * DA_V_DIM
    c_ckv = c_v + MLA_Q_RANK + MLA_KV_RANK
    c_kr = c_ckv + MLA_ROPE
    w_kr = w[:, c_ckv:c_kr]
    w_lat = jnp.concatenate([w[:, c_v:c_ckv], _pad_lanes(w_kr), _pad_lanes(_swap_halves(w_kr))], axis=1)
    n_halves = DA_HEADS * 2
    da_gain = jnp.concatenate([jnp.tile(da_q_norm[l], n_halves) * (DA_HEAD_DIM ** -0.5 * LOG2E),
                               jnp.tile(da_k_norm[l], n_halves)])
    wq = mla_w_uq[l].reshape(MLA_Q_RANK, MLA_HEADS, MLA_QK)
    wq_rope = wq[..., MLA_NOPE:]
    wq = jnp.concatenate([wq[..., :MLA_NOPE], _pad_lanes(wq_rope), _pad_lanes(_swap_halves(wq_rope))], axis=-1)
    wq = jnp.transpose(wq, (1, 0, 2)).astype(BF16)
    wkv = jnp.transpose(mla_w_ukv[l].reshape(MLA_KV_RANK, MLA_HEADS, MLA_NOPE + MLA_V), (1, 0, 2)).astype(BF16)
    gq = (_pad_lanes(mla_q_norm[l]) * (MLA_QK ** -0.5 * LOG2E)).reshape(1, MLA_QK_PAD).astype(F32)
    gk = _pad_lanes(mla_k_norm[l]).reshape(1, MLA_QK_PAD).astype(F32)

    qk = in_proj(x2, attn_norm[l], w[:, :c_k].astype(BF16), col_gain=da_gain).reshape(b, s, c_k)
    va = in_proj(x2, attn_norm[l], w[:, c_k:c_v].astype(BF16)).reshape(b, s, c_v - c_k)
    gates = in_proj(x2, attn_norm[l], w[:, c_kr:].astype(BF16), sigmoid=True)
    cqn, ckvn, kr = latent_proj(x2, attn_norm[l], w_lat.astype(BF16), mla_q_latent_norm[l],
                                mla_kv_latent_norm[l], cos_t, sin_t, s)

    lam = (jnp.exp(jnp.sum(da_lambda_q1[l] * da_lambda_k1[l]))
           - jnp.exp(jnp.sum(da_lambda_q2[l] * da_lambda_k2[l])) + LAMBDA_INIT).reshape(1).astype(F32)
    out_gain = (da_out_norm[l] * (1.0 - LAMBDA_INIT)).reshape(1, DA_V_DIM).astype(F32)
    ya = da_attention(qk, va, _bias_tiles(rel_bias, blk), lam, out_gain, blk)

    qb = mla_q_proj(cqn, wq, cos_t, sin_t, gq, b, s)
    kb, vb = mla_kv_proj(ckvn, wkv, kr, gk, b, s)
    causal = np.arange(blk)[:, None] >= np.arange(blk)[None, :]
    mask_tile = jnp.asarray(np.where(causal, 0.0, MASK_VALUE).astype(np.float32))
    yb = mla_attention(qb, kb, vb, mask_tile, blk)

    x1, ht, qp = merge_and_query(ya.reshape(t, -1), yb.reshape(t, -1), gates, x2, da_proj[l].astype(BF16),
                                 mla_proj[l].astype(BF16), w_out[l].astype(BF16), ffn_norm[l],
                                 peer_w_query[l].astype(BF16))

    n_i, e1, k_j, e2 = peer_select(qp, peer_sub_keys[l])
    u = peer_u[l].astype(BF16)
    vt = jnp.transpose(peer_v[l]).astype(BF16)
    return peer_experts(ht, x1, u, vt, n_i, e1, k_j, e2).reshape(b, s, d)
```
